```python
import math
import jax
import jax.numpy as jnp
from jax import lax
import numpy as np

D_MODEL = 4096
BATCH = 16
SEQ = 256
DEPTH = 4
DEC_BATCH = 4
DEC_SEQ = 1024
PAST_LEN = 256

GRID_W = 64
W_A = D_MODEL // 2
HEAD_A = 64
N_HEADS_A = W_A // HEAD_A
LORA_W = 128
LORA_A = 128
LORA_G = 256
RWKV_COLS = 3 * W_A + 2 * LORA_W + 2 * LORA_A + LORA_G
W_B = D_MODEL // 2
CONV_B = 31
IN_AB = RWKV_COLS + 2 * W_B
HY_SHORT = 3
HY_EMB_DIM = 33
HY_FILTER_HIDDEN = 64
HY_TARGET = 1e-2
HY_FAST_PCT = 0.3
HY_SLOW_PCT = 1.5
HY_MAX_DECAY = math.log(HY_TARGET) / HY_FAST_PCT
HY_MIN_DECAY = math.log(HY_TARGET) / HY_SLOW_PCT
N_EXPERTS = 16
D_EXPERT = 1536
EC_FACTOR = 2
N_AB_LAYERS = (DEPTH + 1) // 2
N_C_LAYERS = DEPTH // 2
RMS_EPS = 1e-6
LN_EPS = 1e-5
GN_EPS = 64e-5

kernel_name = "rwkv7_conformer_hyena_ec_moe_diffusion_step"


def rms_norm(x, g):
    xf = x.astype(jnp.float32)
    y = xf * lax.rsqrt(jnp.mean(xf * xf, axis=-1, keepdims=True) + RMS_EPS)
    return y.astype(x.dtype) * g


def layer_norm(x, g, b):
    xf = x.astype(jnp.float32)
    mu = jnp.mean(xf, axis=-1, keepdims=True)
    var = jnp.mean(jnp.square(xf - mu), axis=-1, keepdims=True)
    return ((xf - mu) * lax.rsqrt(var + LN_EPS)).astype(x.dtype) * g + b


def depthwise_conv(x, w, b):
    k = w.shape[0]
    pad = (k - 1) // 2
    y = lax.conv_general_dilated(x, w[:, None, :].astype(x.dtype), window_strides=(1,),
                                 padding=[(pad, pad)], dimension_numbers=('NWC', 'WIO', 'NWC'),
                                 feature_group_count=x.shape[-1])
    return y + b


def seq_shift(p):
    b, t, c = p.shape
    g = p.reshape(b, t, c // 2, 2)
    prev = jnp.pad(g[:, :-1, :, 0], ((0, 0), (1, 0), (0, 0)))
    nxt = jnp.pad(g[:, 1:, :, 1], ((0, 0), (0, 1), (0, 0)))
    return jnp.stack([prev, nxt], axis=-1).reshape(b, t, c)


def grid_shift(p):
    b, t, c = p.shape
    rows = t // GRID_W
    g = p.reshape(b, rows, GRID_W, c // 4, 4)
    left = jnp.pad(g[:, :, :-1, :, 0], ((0, 0), (0, 0), (1, 0), (0, 0)))
    right = jnp.pad(g[:, :, 1:, :, 1], ((0, 0), (0, 0), (0, 1), (0, 0)))
    up = jnp.pad(g[:, :-1, :, :, 2], ((0, 0), (1, 0), (0, 0), (0, 0)))
    down = jnp.pad(g[:, 1:, :, :, 3], ((0, 0), (0, 1), (0, 0), (0, 0)))
    return jnp.stack([left, right, up, down], axis=-1).reshape(b, t, c)


def rwkv_step(s, inp):
    r, w, k, v, kk, a = inp
    sa = jnp.einsum('bzhvk,bzhk->bzhv', s, kk)
    s = s * w[..., None, :] - sa[..., None] * (kk * a)[..., None, :] + v[..., None] * k[..., None, :]
    y = jnp.einsum('bzhvk,bzhk->bzhv', s, r)
    return s, y


def hyena_filters(length, w1, b1, w2, b2, w3, b3, w4, freq):
    f32 = jnp.float32
    t = jnp.linspace(0.0, 1.0, length, dtype=f32)[:, None]
    bands = (HY_EMB_DIM - 1) // 2
    ang = (2.0 * math.pi / length) * jnp.arange(length, dtype=f32)[:, None] \
        * jnp.linspace(1e-4, bands - 1, bands, dtype=f32)[None, :]
    z = jnp.concatenate([t, jnp.cos(ang), -jnp.sin(ang)], axis=-1)
    fr = freq.astype(f32)
    h = jnp.sin(fr * (z @ w1.astype(f32) + b1.astype(f32)))
    h = jnp.sin(fr * (h @ w2.astype(f32) + b2.astype(f32)))
    h = jnp.sin(fr * (h @ w3.astype(f32) + b3.astype(f32)))
    h = (h @ w4.astype(f32)).reshape(length, 2, D_MODEL)
    deltas = jnp.abs(jnp.linspace(HY_MIN_DECAY, HY_MAX_DECAY, D_MODEL, dtype=f32))
    h = h * jnp.exp(-t * deltas)[:, None, :]
    h = h / (jnp.sum(jnp.abs(h), axis=(0, 1)) + 1e-6)
    return h[:, 0], h[:, 1]


def long_conv_bidir(z, hf, hb):
    length = z.shape[1]
    kern = jnp.concatenate([hf, jnp.zeros((1, hf.shape[1]), hf.dtype), jnp.flip(hb[1:], axis=0)], axis=0)
    zf = jnp.fft.rfft(z.astype(jnp.float32), n=2 * length, axis=1)
    kf = jnp.fft.rfft(kern, n=2 * length, axis=0)
    return jnp.fft.irfft(zf * kf[None], n=2 * length, axis=1)[:, :length]


def ec_moe(xn, router, wg, wu, wd):
    b, t, d = xn.shape
    cap = EC_FACTOR * t // N_EXPERTS
    aff = jax.nn.softmax(jnp.einsum('btd,de->bte', xn, router).astype(jnp.float32), axis=-1)
    gate, idx = lax.top_k(jnp.swapaxes(aff, 1, 2), cap)
    xg = jax.vmap(lambda xb, ib: xb[ib])(xn, idx)
    hdn = jax.nn.silu(jnp.einsum('becd,edf->becf', xg, wg)) * jnp.einsum('becd,edf->becf', xg, wu)
    out = jnp.einsum('becf,efd->becd', hdn, wd) * gate[..., None].astype(xn.dtype)
    return jax.vmap(lambda ib, ob: jnp.zeros((t, d), ob.dtype).at[ib.reshape(-1)].add(ob.reshape(-1, d)))(idx, out)


def setup_inputs(seed: int = 0) -> dict:
    key = jax.random.key(seed)
    ks = iter(jax.random.split(key, 64))
    f32 = jnp.float32
    D = D_MODEL

    def nrm(shape, scale):
        return jax.random.normal(next(ks), shape, f32) * scale

    def near_one(shape):
        return 1.0 + nrm(shape, 0.05)

    return {
        'x_prompt': nrm((BATCH, SEQ, D), 1.0),
        'x_sample': nrm((DEC_BATCH, DEC_SEQ, D), 1.0),
        'state_rwkv': nrm((DEC_BATCH, N_AB_LAYERS, 2, N_HEADS_A, HEAD_A, HEAD_A), 1.0),
        'c': nrm((DEC_BATCH, D), 1.0),
        'c_ctx': nrm((D,), 1.0),
        'norm_g': near_one((DEPTH, 2, D)),
        'mod_w': nrm((DEPTH, D, 6 * D), 0.5 * D ** -0.5),
        'mod_b': nrm((DEPTH, 6 * D), 0.02),
        'ab_w_in': nrm((N_AB_LAYERS, D, IN_AB), D ** -0.5),
        'ab_mu': jax.random.uniform(next(ks), (N_AB_LAYERS, RWKV_COLS), f32),
        'ab_w_up': nrm((N_AB_LAYERS, 2, LORA_W, W_A), 0.5 * LORA_W ** -0.5),
        'ab_w0': jax.random.uniform(next(ks), (N_AB_LAYERS, 2, W_A), f32, -6.5, -1.5),
        'ab_a_up': nrm((N_AB_LAYERS, 2, LORA_A, W_A), 0.5 * LORA_A ** -0.5),
        'ab_a0': nrm((N_AB_LAYERS, 2, W_A), 0.1),
        'ab_g_up': nrm((N_AB_LAYERS, LORA_G, W_A), LORA_G ** -0.5),
        'ab_k_k': 0.85 + nrm((N_AB_LAYERS, W_A), 0.02),
        'ab_k_a': near_one((N_AB_LAYERS, W_A)),
        'ab_r_k': nrm((N_AB_LAYERS, N_HEADS_A, HEAD_A), 0.1),
        'ab_lnx_g': near_one((N_AB_LAYERS, W_A)),
        'ab_lnx_b': nrm((N_AB_LAYERS, W_A), 0.02),
        'ab_dw_w': nrm((N_AB_LAYERS, CONV_B, W_B), CONV_B ** -0.5),
        'ab_dw_b': nrm((N_AB_LAYERS, W_B), 0.02),
        'ab_ln_g': near_one((N_AB_LAYERS, W_B)),
        'ab_ln_b': nrm((N_AB_LAYERS, W_B), 0.02),
        'ab_w_out': nrm((N_AB_LAYERS, W_A + W_B, D), (W_A + W_B) ** -0.5),
        'hy_w_in': nrm((N_C_LAYERS, D, 3 * D), D ** -0.5),
        'hy_sc_w': nrm((N_C_LAYERS, HY_SHORT, 3 * D), HY_SHORT ** -0.5),
        'hy_sc_b': nrm((N_C_LAYERS, 3 * D), 0.02),
        'hy_f_w1': nrm((N_C_LAYERS, HY_EMB_DIM, HY_FILTER_HIDDEN), HY_EMB_DIM ** -0.5),
        'hy_f_b1': nrm((N_C_LAYERS, HY_FILTER_HIDDEN), 0.1),
        'hy_f_w2': nrm((N_C_LAYERS, HY_FILTER_HIDDEN, HY_FILTER_HIDDEN), HY_FILTER_HIDDEN ** -0.5),
        'hy_f_b2': nrm((N_C_LAYERS, HY_FILTER_HIDDEN), 0.1),
        'hy_f_w3': nrm((N_C_LAYERS, HY_FILTER_HIDDEN, HY_FILTER_HIDDEN), HY_FILTER_HIDDEN ** -0.5),
        'hy_f_b3': nrm((N_C_LAYERS, HY_FILTER_HIDDEN), 0.1),
        'hy_f_w4': nrm((N_C_LAYERS, HY_FILTER_HIDDEN, 2 * D), HY_FILTER_HIDDEN ** -0.5),
        'hy_f_freq': near_one((N_C_LAYERS, HY_FILTER_HIDDEN)),
        'hy_bias': nrm((N_C_LAYERS, D), 1.0),
        'hy_w_out': nrm((N_C_LAYERS, D, D), D ** -0.5),
        'moe_router': nrm((DEPTH, D, N_EXPERTS), D ** -0.5),
        'moe_w_gate': nrm((DEPTH, N_EXPERTS, D, D_EXPERT), D ** -0.5),
        'moe_w_up': nrm((DEPTH, N_EXPERTS, D, D_EXPERT), D ** -0.5),
        'moe_w_down': nrm((DEPTH, N_EXPERTS, D_EXPERT, D), D_EXPERT ** -0.5),
        'final_g': near_one((D,)),
    }


def reference(x_prompt, x_sample, state_rwkv, c, c_ctx,
              norm_g, mod_w, mod_b,
              ab_w_in, ab_mu, ab_w_up, ab_w0, ab_a_up, ab_a0, ab_g_up, ab_k_k, ab_k_a, ab_r_k,
              ab_lnx_g, ab_lnx_b, ab_dw_w, ab_dw_b, ab_ln_g, ab_ln_b, ab_w_out,
              hy_w_in, hy_sc_w, hy_sc_b, hy_f_w1, hy_f_b1, hy_f_w2, hy_f_b2, hy_f_w3, hy_f_b3,
              hy_f_w4, hy_f_freq, hy_bias, hy_w_out,
              moe_router, moe_w_gate, moe_w_up, moe_w_down, final_g):
    f32 = jnp.float32

    def heads(z):
        return z.reshape(z.shape[:-1] + (N_HEADS_A, HEAD_A))

    def dir_major(z):
        z = jnp.stack([z[:, :, 0], jnp.flip(z[:, :, 1], axis=1)], axis=2)
        return jnp.moveaxis(z, 1, 0)

    def both(z):
        return jnp.stack([z, z], axis=2)

    def rwkv_mix(p, i, s0, on_grid):
        b, t, _ = p.shape
        dt = p.dtype
        shifted = grid_shift(p) if on_grid else seq_shift(p)
        p = p + (shifted - p) * ab_mu[i]
        r = p[..., :W_A]
        k = p[..., W_A:2 * W_A]
        v = p[..., 2 * W_A:3 * W_A]
        o = 3 * W_A
        wd = p[..., o:o + 2 * LORA_W].reshape(b, t, 2, LORA_W)
        o += 2 * LORA_W
        ad = p[..., o:o + 2 * LORA_A].reshape(b, t, 2, LORA_A)
        o += 2 * LORA_A
        gd = p[..., o:o + LORA_G]
        wl = (ab_w0[i] + jnp.einsum('btzl,zlc->btzc', jnp.tanh(wd), ab_w_up[i])).astype(f32)
        decay = jnp.exp(-jnp.exp(-jax.nn.softplus(-wl) - 0.5))
        a = jax.nn.sigmoid((ab_a0[i] + jnp.einsum('btzl,zlc->btzc', ad, ab_a_up[i])).astype(f32))
        g = jnp.matmul(jax.nn.sigmoid(gd), ab_g_up[i])
        kk = heads((k * ab_k_k[i]).astype(f32))
        kk = kk / jnp.maximum(jnp.sqrt(jnp.sum(kk * kk, axis=-1, keepdims=True)), 1e-12)
        kd = heads(k.astype(f32)[:, :, None, :] * (1.0 + (a - 1.0) * ab_k_a[i]))
        rh = heads(r.astype(f32))
        vh = heads(v.astype(f32))
        inputs = (dir_major(both(rh)), dir_major(heads(decay)), dir_major(kd),
                  dir_major(both(vh)), dir_major(both(kk)), dir_major(heads(a)))
        s_fin, ys = lax.scan(rwkv_step, s0.astype(f32), inputs)
        ys = jnp.moveaxis(ys, 0, 1)
        y = ys[:, :, 0] + jnp.flip(ys[:, :, 1], axis=1)
        mu = jnp.mean(y, axis=-1, keepdims=True)
        var = jnp.mean(jnp.square(y - mu), axis=-1, keepdims=True)
        y = ((y - mu) * lax.rsqrt(var + GN_EPS)).reshape(b, t, W_A) * ab_lnx_g[i] + ab_lnx_b[i]
        bonus = jnp.sum(jnp.sum(rh[:, :, None] * kd * ab_r_k[i], axis=-1, keepdims=True) * vh[:, :, None], axis=2)
        y = (y + bonus.reshape(b, t, W_A)) * g
        return y.astype(dt), s_fin

    def ab_mixer(h, i, s0, on_grid):
        proj = jnp.matmul(h, ab_w_in[i])
        y_a, s_fin = rwkv_mix(proj[..., :RWKV_COLS], i, s0, on_grid)
        pa = proj[..., RWKV_COLS:RWKV_COLS + W_B]
        pb = proj[..., RWKV_COLS + W_B:]
        cv = depthwise_conv(pa * jax.nn.sigmoid(pb), ab_dw_w[i], ab_dw_b[i])
        cv = jax.nn.silu(layer_norm(cv, ab_ln_g[i], ab_ln_b[i]))
        return jnp.matmul(jnp.concatenate([y_a, cv], axis=-1), ab_w_out[i]), s_fin

    def hyena_mixer(h, i):
        length = h.shape[1]
        u = depthwise_conv(jnp.matmul(h, hy_w_in[i]), hy_sc_w[i], hy_sc_b[i])
        x0, x1, v = jnp.split(u, 3, axis=-1)
        hf, hb = hyena_filters(length, hy_f_w1[i], hy_f_b1[i], hy_f_w2[i], hy_f_b2[i],
                               hy_f_w3[i], hy_f_b3[i], hy_f_w4[i], hy_f_freq[i])
        z = v * x1
        y = long_conv_bidir(z, hf, hb).astype(h.dtype) + hy_bias[i] * z
        return jnp.matmul(y * x0, hy_w_out[i])

    def run(x, cond, on_grid, s_init):
        states = []
        for l in range(DEPTH):
            mod = (jnp.matmul(jax.nn.silu(cond), mod_w[l]) + mod_b[l])[:, None, :]
            sh1, sc1, g1, sh2, sc2, g2 = jnp.split(mod, 6, axis=-1)
            h = rms_norm(x, norm_g[l, 0]) * (1.0 + sc1) + sh1
            if l % 2 == 0:
                y, s_fin = ab_mixer(h, l // 2, s_init[:, l // 2], on_grid)
                states.append(s_fin)
            else:
                y = hyena_mixer(h, l // 2)
            x = x + g1 * y
            h = rms_norm(x, norm_g[l, 1]) * (1.0 + sc2) + sh2
            x = x + g2 * ec_moe(h, moe_router[l], moe_w_gate[l], moe_w_up[l], moe_w_down[l])
        return rms_norm(x, final_g), states

    zero_state = jnp.zeros((x_prompt.shape[0], N_AB_LAYERS, 2, N_HEADS_A, HEAD_A, HEAD_A), f32)
    y_prompt, states_prompt = run(x_prompt, c_ctx[None, :], False, zero_state)
    new_state_rwkv = jnp.stack(states_prompt, axis=1).astype(x_prompt.dtype)
    y_sample, _ = run(x_sample, c, True, state_rwkv)
    return (y_prompt, y_sample, new_state_rwkv)
```

```python
import functools
import math

import jax
import jax.numpy as jnp
from jax import lax
from jax.experimental import pallas as pl
from jax.experimental.pallas import tpu as pltpu

F32 = jnp.float32
BF16 = jnp.bfloat16

D_MODEL = 4096
N_PROMPT, T_PROMPT = 16, 256
N_SAMPLE, T_SAMPLE = 4, 1024
GRID_W = 64
DEPTH = 4
W_A = D_MODEL // 2
HEAD = 64
N_HEADS = W_A // HEAD
LORA_W = 128
LORA_A = 128
LORA_G = 256
RWKV_COLS = 3 * W_A + 2 * LORA_W + 2 * LORA_A + LORA_G
W_B = D_MODEL // 2
CONV_B = 31
HY_EMB_DIM = 33
HY_TARGET = 1e-2
HY_MAX_DECAY = math.log(HY_TARGET) / 0.3
HY_MIN_DECAY = math.log(HY_TARGET) / 1.5
N_EXPERTS = 16
D_EXPERT = 1536
EC_FACTOR = 2
RMS_EPS = 1e-6
LN_EPS = 1e-5
GN_EPS = 64e-5

ROW_BLOCK = 256
N_TOK = N_PROMPT * T_PROMPT + N_SAMPLE * T_SAMPLE
N_COND = 8
LANES = 128
VMEM_LIMIT = 56 * 1024 * 1024


def _cparams(sem):
    return pltpu.CompilerParams(dimension_semantics=sem, vmem_limit_bytes=VMEM_LIMIT)


def _mm_kernel(a_ref, w_ref, o_ref):
    a = a_ref[0].astype(BF16)
    w = w_ref[0].astype(BF16)
    o_ref[0] = jnp.dot(a, w, preferred_element_type=F32).astype(o_ref.dtype)


def _matmul(a, w, *, tm, tn, out_dtype=F32):
    ba, m, k = a.shape
    bw, k2, n = w.shape
    assert k == k2 and m % tm == 0 and n % tn == 0
    b = max(ba, bw)
    a_idx = (lambda bi, i, j: (bi, i, 0)) if ba > 1 else (lambda bi, i, j: (0, i, 0))
    w_idx = (lambda bi, i, j: (bi, 0, j)) if bw > 1 else (lambda bi, i, j: (0, 0, j))
    return pl.pallas_call(
        _mm_kernel,
        grid=(b, m // tm, n // tn),
        in_specs=[pl.BlockSpec((1, tm, k), a_idx), pl.BlockSpec((1, k, tn), w_idx)],
        out_specs=pl.BlockSpec((1, tm, tn), lambda bi, i, j: (bi, i, j)),
        out_shape=jax.ShapeDtypeStruct((b, m, n), out_dtype),
        compiler_params=_cparams(("parallel", "parallel", "arbitrary")),
        name="matmul",
    )(a, w)


def _mm2(a, w, *, tm, tn, out_dtype=F32):
    return _matmul(a[None], w[None], tm=tm, tn=tn, out_dtype=out_dtype)[0]


def _mod_kernel(c_ref, w_ref, b_ref, o_ref):
    c = c_ref[...]
    act = c * jax.nn.sigmoid(c)
    o_ref[...] = jnp.dot(act, w_ref[...], precision=lax.Precision.HIGHEST,
                         preferred_element_type=F32) + b_ref[...]


def _modulation(cond, w, b, *, tn=512):
    n = w.shape[1]
    return pl.pallas_call(
        _mod_kernel,
        grid=(n // tn,),
        in_specs=[pl.BlockSpec((N_COND, D_MODEL), lambda j: (0, 0)),
                  pl.BlockSpec((D_MODEL, tn), lambda j: (0, j)),
                  pl.BlockSpec((1, tn), lambda j: (0, j))],
        out_specs=pl.BlockSpec((N_COND, tn), lambda j: (0, j)),
        out_shape=jax.ShapeDtypeStruct((N_COND, n), F32),
        compiler_params=_cparams(("arbitrary",)),
        name="modulation",
    )(cond, w, b[None])


def _norm_kernel(cidx_ref, x_ref, g_ref, sc_ref, sh_ref, o_ref):
    x = x_ref[...]
    y = x * lax.rsqrt(jnp.mean(x * x, axis=-1, keepdims=True) + RMS_EPS)
    o_ref[...] = ((y * g_ref[...]) * (1.0 + sc_ref[0]) + sh_ref[0]).astype(o_ref.dtype)


def _norm_router_kernel(cidx_ref, x_ref, g_ref, sc_ref, sh_ref, r_ref, o_ref, l_ref):
    x = x_ref[...]
    y = x * lax.rsqrt(jnp.mean(x * x, axis=-1, keepdims=True) + RMS_EPS)
    h = (y * g_ref[...]) * (1.0 + sc_ref[0]) + sh_ref[0]
    o_ref[...] = h.astype(o_ref.dtype)
    l_ref[...] = jnp.dot(h, r_ref[...], precision=lax.Precision.HIGHEST, preferred_element_type=F32)


def _norm_mod(x, g, sc, sh, cidx, *, out_dtype, router=None):
    nblk = N_TOK // ROW_BLOCK
    row = pl.BlockSpec((ROW_BLOCK, D_MODEL), lambda i, c: (i, 0))
    vec = pl.BlockSpec((1, D_MODEL), lambda i, c: (0, 0))
    sel = pl.BlockSpec((1, 1, D_MODEL), lambda i, c: (c[i], 0, 0))
    in_specs = [row, vec, sel, sel]
    args = [x, g[None], sc[:, None, :], sh[:, None, :]]
    out_specs = row
    out_shape = jax.ShapeDtypeStruct((N_TOK, D_MODEL), out_dtype)
    body = _norm_kernel
    if router is not None:
        in_specs.append(pl.BlockSpec((D_MODEL, N_EXPERTS), lambda i, c: (0, 0)))
        args.append(router)
        out_specs = [row, pl.BlockSpec((ROW_BLOCK, N_EXPERTS), lambda i, c: (i, 0))]
        out_shape = [out_shape, jax.ShapeDtypeStruct((N_TOK, N_EXPERTS), F32)]
        body = _norm_router_kernel
    return pl.pallas_call(
        body,
        grid_spec=pltpu.PrefetchScalarGridSpec(
            num_scalar_prefetch=1, grid=(nblk,), in_specs=in_specs, out_specs=out_specs),
        out_shape=out_shape,
        compiler_params=_cparams(("arbitrary",)),
        name="norm_mod",
    )(cidx, *args)


def _scan_kernel(r_ref, w_ref, k_ref, v_ref, kk_ref, a_ref, s0_ref, y_ref, sfin_ref, s_scr, *, tb, nt):
    d = pl.program_id(0)
    j = pl.program_id(2)

    @pl.when(j == 0)
    def _():
        s_scr[...] = s0_ref[0, 0]

    def step(tt, carry):
        t = tt + d * (tb - 1 - 2 * tt)
        kk = kk_ref[0, t]
        w = w_ref[0, 0, t]
        b = kk * a_ref[0, 0, t]
        kd = k_ref[0, 0, t]
        r = r_ref[0, t]

        def per_value(i, c):
            si = s_scr[i]
            sa = jnp.sum(si * kk, axis=0, keepdims=True)
            vi = v_ref[0, t, pl.ds(i, 1), :]
            sn = si * w - sa * b + vi * kd
            s_scr[i] = sn
            y_ref[0, 0, t, pl.ds(i, 1), :] = jnp.sum(sn * r, axis=0, keepdims=True)
            return c

        return lax.fori_loop(0, HEAD, per_value, carry, unroll=2)

    lax.fori_loop(0, tb, step, 0)

    @pl.when(j == nt - 1)
    def _():
        sfin_ref[0, 0] = s_scr[...]


def _rwkv_scan(r, w, k, v, kk, a, s0, *, tb=32):
    g, t = r.shape[0], r.shape[1]
    nt = t // tb

    def tmap(d, j):
        return j + d * (nt - 1 - 2 * j)

    shared = pl.BlockSpec((1, tb, HEAD, LANES), lambda d, gi, j: (gi, tmap(d, j), 0, 0))
    perdir = pl.BlockSpec((1, 1, tb, HEAD, LANES), lambda d, gi, j: (d, gi, tmap(d, j), 0, 0))
    state = pl.BlockSpec((1, 1, HEAD, HEAD, LANES), lambda d, gi, j: (d, gi, 0, 0, 0))
    return pl.pallas_call(
        functools.partial(_scan_kernel, tb=tb, nt=nt),
        grid=(2, g, nt),
        in_specs=[shared, perdir, perdir, shared, shared, perdir, state],
        out_specs=[perdir, state],
        out_shape=[jax.ShapeDtypeStruct((2, g, t, HEAD, LANES), F32),
                   jax.ShapeDtypeStruct((2, g, HEAD, HEAD, LANES), F32)],
        scratch_shapes=[pltpu.VMEM((HEAD, HEAD, LANES), F32)],
        compiler_params=_cparams(("parallel", "parallel", "arbitrary")),
        name="rwkv_scan",
    )(r, w, k, v, kk, a, s0)


def _moe_up_kernel(x_ref, wg_ref, wu_ref, h_ref):
    x = x_ref[0]
    g = jnp.dot(x, wg_ref[0].astype(BF16), preferred_element_type=F32)
    u = jnp.dot(x, wu_ref[0].astype(BF16), preferred_element_type=F32)
    h_ref[0] = (g * jax.nn.sigmoid(g) * u).astype(h_ref.dtype)


def _moe_down_kernel(h_ref, wd_ref, gate_ref, o_ref):
    o_ref[0] = jnp.dot(h_ref[0], wd_ref[0].astype(BF16), preferred_element_type=F32) * gate_ref[0]


def _moe_experts(xg, gate, wg, wu, wd, *, tf=256, tn=512):
    e, rows, _ = xg.shape
    h = pl.pallas_call(
        _moe_up_kernel,
        grid=(e, D_EXPERT // tf),
        in_specs=[pl.BlockSpec((1, rows, D_MODEL), lambda ei, f: (ei, 0, 0)),
                  pl.BlockSpec((1, D_MODEL, tf), lambda ei, f: (ei, 0, f)),
                  pl.BlockSpec((1, D_MODEL, tf), lambda ei, f: (ei, 0, f))],
        out_specs=pl.BlockSpec((1, rows, tf), lambda ei, f: (ei, 0, f)),
        out_shape=jax.ShapeDtypeStruct((e, rows, D_EXPERT), BF16),
        compiler_params=_cparams(("parallel", "arbitrary")),
        name="moe_up",
    )(xg, wg, wu)
    return pl.pallas_call(
        _moe_down_kernel,
        grid=(e, D_MODEL // tn),
        in_specs=[pl.BlockSpec((1, rows, D_EXPERT), lambda ei, n: (ei, 0, 0)),
                  pl.BlockSpec((1, D_EXPERT, tn), lambda ei, n: (ei, 0, n)),
                  pl.BlockSpec((1, rows, 1), lambda ei, n: (ei, 0, 0))],
        out_specs=pl.BlockSpec((1, rows, tn), lambda ei, n: (ei, 0, n)),
        out_shape=jax.ShapeDtypeStruct((e, rows, D_MODEL), F32),
        compiler_params=_cparams(("parallel", "arbitrary")),
        name="moe_down",
    )(h, wd, gate)


def _split_runs(x):
    return (x[:N_PROMPT * T_PROMPT].reshape(N_PROMPT, T_PROMPT, -1),
            x[N_PROMPT * T_PROMPT:].reshape(N_SAMPLE, T_SAMPLE, -1))


def _merge_runs(p, s):
    return jnp.concatenate([p.reshape(N_PROMPT * T_PROMPT, -1), s.reshape(N_SAMPLE * T_SAMPLE, -1)], axis=0)


def _shift_time(p, off):
    if off > 0:
        return jnp.pad(p[:, :-off], ((0, 0), (off, 0), (0, 0)))
    return jnp.pad(p[:, -off:], ((0, 0), (0, -off), (0, 0)))


def _token_shift(p, on_grid):
    ch = lax.broadcasted_iota(jnp.int32, (1, 1, p.shape[-1]), 2)
    if not on_grid:
        return jnp.where(ch % 2 == 0, _shift_time(p, 1), _shift_time(p, -1))
    col = lax.broadcasted_iota(jnp.int32, (1, p.shape[1], 1), 1) % GRID_W
    left = jnp.where(col == 0, 0.0, _shift_time(p, 1))
    right = jnp.where(col == GRID_W - 1, 0.0, _shift_time(p, -1))
    up = _shift_time(p, GRID_W)
    down = _shift_time(p, -GRID_W)
    q = ch % 4
    return jnp.where(q == 0, left, jnp.where(q == 1, right, jnp.where(q == 2, up, down)))


def _depthwise(x, w, b):
    k = w.shape[0]
    pad = (k - 1) // 2
    acc = None
    for j in range(k):
        term = _shift_time(x, pad - j) * w[j]
        acc = term if acc is None else acc + term
    return acc + b


def _to_chains(x, nb, t):
    hg = LANES // nb
    g = N_HEADS // hg
    x = x.reshape(nb, t, g, hg, HEAD)
    return jnp.transpose(x, (2, 1, 4, 3, 0)).reshape(g, t, HEAD, LANES)


def _from_chains(y, nb, t):
    hg = LANES // nb
    g = N_HEADS // hg
    y = y.reshape(g, t, HEAD, hg, nb)
    return jnp.transpose(y, (4, 1, 0, 3, 2)).reshape(nb * t, W_A)


def _state_to_chains(s, nb):
    hg = LANES // nb
    g = N_HEADS // hg
    s = s.reshape(nb, 2, g, hg, HEAD, HEAD)
    return jnp.transpose(s, (1, 2, 4, 5, 3, 0)).reshape(2, g, HEAD, HEAD, LANES)


def _state_from_chains(s, nb):
    hg = LANES // nb
    g = N_HEADS // hg
    s = s.reshape(2, g, HEAD, HEAD, hg, nb)
    return jnp.transpose(s, (5, 0, 1, 4, 2, 3)).reshape(nb, 2, N_HEADS, HEAD, HEAD)


def _heads(z):
    return z.reshape(z.shape[:-1] + (N_HEADS, HEAD))


def _rwkv_mix(p_prompt, p_sample, s0_sample, prm):
    mixed = []
    for p, on_grid in ((p_prompt, False), (p_sample, True)):
        mixed.append(p + (_token_shift(p, on_grid) - p) * prm["mu"])
    p = _merge_runs(*mixed)
    r = p[:, :W_A]
    k = p[:, W_A:2 * W_A]
    v = p[:, 2 * W_A:3 * W_A]
    o = 3 * W_A
    wd = jnp.tanh(p[:, o:o + 2 * LORA_W]).astype(BF16)
    o += 2 * LORA_W
    ad = p[:, o:o + 2 * LORA_A].astype(BF16)
    o += 2 * LORA_A
    gd = jax.nn.sigmoid(p[:, o:o + LORA_G]).astype(BF16)

    wl = _matmul(jnp.stack([wd[:, :LORA_W], wd[:, LORA_W:]]), prm["w_up"], tm=2048, tn=512) + prm["w0"][:, None, :]
    decay = jnp.exp(-jnp.exp(-jax.nn.softplus(-wl) - 0.5))
    a = jax.nn.sigmoid(
        _matmul(jnp.stack([ad[:, :LORA_A], ad[:, LORA_A:]]), prm["a_up"], tm=2048, tn=512) + prm["a0"][:, None, :])
    g = _mm2(gd, prm["g_up"], tm=2048, tn=512)

    kk = _heads(k * prm["k_k"])
    kk = (kk / jnp.maximum(jnp.sqrt(jnp.sum(kk * kk, axis=-1, keepdims=True)), 1e-12)).reshape(N_TOK, W_A)
    kd = k[None] * (1.0 + (a - 1.0) * prm["k_a"])

    n_p = N_PROMPT * T_PROMPT
    ys, s_prompt = [], None
    for lo, hi, nb, t, s0 in ((0, n_p, N_PROMPT, T_PROMPT, None), (n_p, N_TOK, N_SAMPLE, T_SAMPLE, s0_sample)):
        tc = functools.partial(_to_chains, nb=nb, t=t)
        both = lambda z: jnp.stack([tc(z[0, lo:hi]), tc(z[1, lo:hi])])
        if s0 is None:
            s0c = jnp.zeros((2, N_HEADS * nb // LANES, HEAD, HEAD, LANES), F32)
        else:
            s0c = _state_to_chains(s0, nb)
        y2, sf = _rwkv_scan(tc(r[lo:hi]), both(decay), both(kd), tc(v[lo:hi]), tc(kk[lo:hi]), both(a), s0c)
        ys.append(_from_chains(y2[0] + y2[1], nb, t))
        if s0 is None:
            s_prompt = _state_from_chains(sf, nb)
    y = _heads(jnp.concatenate(ys, axis=0))
    mu = jnp.mean(y, axis=-1, keepdims=True)
    var = jnp.mean(jnp.square(y - mu), axis=-1, keepdims=True)
    y = ((y - mu) * lax.rsqrt(var + GN_EPS)).reshape(N_TOK, W_A) * prm["lnx_g"] + prm["lnx_b"]
    rh, vh = _heads(r), _heads(v)
    bonus = jnp.sum(jnp.sum(rh[None] * _heads(kd) * prm["r_k"], axis=-1, keepdims=True) * vh[None], axis=0)
    y = (y + bonus.reshape(N_TOK, W_A)) * g
    return y.astype(BF16), s_prompt


def _conformer(pa, pb, prm):
    cv = _depthwise(pa * jax.nn.sigmoid(pb), prm["dw_w"], prm["dw_b"])
    mu = jnp.mean(cv, axis=-1, keepdims=True)
    var = jnp.mean(jnp.square(cv - mu), axis=-1, keepdims=True)
    cv = (cv - mu) * lax.rsqrt(var + LN_EPS) * prm["ln_g"] + prm["ln_b"]
    return (cv * jax.nn.sigmoid(cv)).astype(BF16).reshape(-1, W_B)


def _hyena_filters(length, prm):
    hi = lax.Precision.HIGHEST
    t = jnp.linspace(0.0, 1.0, length, dtype=F32)[:, None]
    bands = (HY_EMB_DIM - 1) // 2
    ang = (2.0 * math.pi / length) * jnp.arange(length, dtype=F32)[:, None] \
        * jnp.linspace(1e-4, bands - 1, bands, dtype=F32)[None, :]
    z = jnp.concatenate([t, jnp.cos(ang), -jnp.sin(ang)], axis=-1)
    fr = prm["f_freq"]
    h = jnp.sin(fr * (jnp.dot(z, prm["f_w1"], precision=hi) + prm["f_b1"]))
    h = jnp.sin(fr * (jnp.dot(h, prm["f_w2"], precision=hi) + prm["f_b2"]))
    h = jnp.sin(fr * (jnp.dot(h, prm["f_w3"], precision=hi) + prm["f_b3"]))
    h = jnp.dot(h, prm["f_w4"], precision=hi).reshape(length, 2, D_MODEL)
    deltas = jnp.abs(jnp.linspace(HY_MIN_DECAY, HY_MAX_DECAY, D_MODEL, dtype=F32))
    h = h * jnp.exp(-t * deltas)[:, None, :]
    h = h / (jnp.sum(jnp.abs(h), axis=(0, 1)) + 1e-6)
    return h[:, 0], h[:, 1]


def _dft_tables(length):
    n = 2 * length
    kidx = jnp.arange(length, dtype=jnp.int32)[:, None]
    sidx = jnp.arange(n, dtype=jnp.int32)[None, :]
    ang = (2.0 * math.pi / n) * ((kidx * sidx) % n).astype(F32)
    cos = jnp.cos(ang)
    sin = jnp.sin(ang)
    nyq = jnp.where(sidx % 2 == 0, 1.0, -1.0).astype(F32)
    sin = jnp.where(kidx == 0, nyq, sin)
    fwd = jnp.concatenate([cos, sin], axis=0)
    scale = jnp.where(kidx == 0, 1.0 / n, 2.0 / n)
    inv = jnp.concatenate([(cos * scale)[:, :length].T, (sin * scale)[:, :length].T], axis=1)
    return fwd, inv


def _long_conv(z, hf, hb):
    _, length, _ = z.shape
    fwd, inv = _dft_tables(length)
    kern = jnp.concatenate([hf, jnp.zeros((1, D_MODEL), F32), jnp.flip(hb[1:], axis=0)], axis=0)
    kf = jnp.dot(fwd, kern, precision=lax.Precision.HIGHEST)
    kc, ks = kf[:length], kf[length:]
    zf = _matmul(fwd[None, :, :length], z, tm=min(2 * length, 1024), tn=1024)
    zc, zs = zf[:, :length], zf[:, length:]
    first = (lax.broadcasted_iota(jnp.int32, (1, length, 1), 1) == 0)
    yc = jnp.where(first, zc * kc, zc * kc - zs * ks)
    ys = jnp.where(first, zs * ks, zc * ks + zs * kc)
    yf = jnp.concatenate([yc, ys], axis=1)
    return _matmul(inv[None], yf, tm=min(length, 1024), tn=1024)


def _hyena(u_prompt, u_sample, prm):
    outs = []
    for u in (u_prompt, u_sample):
        u = _depthwise(u, prm["sc_w"], prm["sc_b"])
        x0, x1, v = jnp.split(u, 3, axis=-1)
        hf, hb = _hyena_filters(u.shape[1], prm)
        z = v * x1
        y = _long_conv(z, hf, hb) + prm["bias"] * z
        outs.append((y * x0).astype(BF16))
    return _merge_runs(*outs)


def _ec_moe(h, logits, wg, wu, wd):
    aff = jax.nn.softmax(logits, axis=-1)
    xgs, gates, idxs = [], [], []
    n_p = N_PROMPT * T_PROMPT
    for lo, nb, t in ((0, N_PROMPT, T_PROMPT), (n_p, N_SAMPLE, T_SAMPLE)):
        cap = EC_FACTOR * t // N_EXPERTS
        a = jnp.swapaxes(aff[lo:lo + nb * t].reshape(nb, t, N_EXPERTS), 1, 2)
        gate, idx = lax.top_k(a, cap)
        flat = idx + (lo + jnp.arange(nb, dtype=jnp.int32) * t)[:, None, None]
        flat = jnp.swapaxes(flat, 0, 1).reshape(N_EXPERTS, nb * cap)
        idxs.append(flat)
        gates.append(jnp.swapaxes(gate, 0, 1).reshape(N_EXPERTS, nb * cap))
    idx = jnp.concatenate(idxs, axis=1)
    gate = jnp.concatenate(gates, axis=1)
    xg = h[idx]
    out = _moe_experts(xg, gate[..., None], wg, wu, wd)
    return jnp.zeros((N_TOK, D_MODEL), F32).at[idx.reshape(-1)].add(out.reshape(-1, D_MODEL))


def _expand_rows(m):
    cidx = _cond_index()
    return jnp.repeat(m[cidx], ROW_BLOCK, axis=0)


def _cond_index():
    blocks_p = N_PROMPT * T_PROMPT // ROW_BLOCK
    per_s = T_SAMPLE // ROW_BLOCK
    return jnp.concatenate([jnp.zeros((blocks_p,), jnp.int32),
                            1 + jnp.repeat(jnp.arange(N_SAMPLE, dtype=jnp.int32), per_s)])


def kernel(x_prompt, x_sample, state_rwkv, c, c_ctx, norm_g, mod_w, mod_b, ab_w_in, ab_mu, ab_w_up, ab_w0, ab_a_up, ab_a0, ab_g_up, ab_k_k, ab_k_a, ab_r_k, ab_lnx_g, ab_lnx_b, ab_dw_w, ab_dw_b, ab_ln_g, ab_ln_b, ab_w_out, hy_w_in, hy_sc_w, hy_sc_b, hy_f_w1, hy_f_b1, hy_f_w2, hy_f_b2, hy_f_w3, hy_f_b3, hy_f_w4, hy_f_freq, hy_bias, hy_w_out, moe_router, moe_w_gate, moe_w_up, moe_w_down, final_g):
    x = _merge_runs(x_prompt, x_sample)
    cond = jnp.concatenate([c_ctx[None], c, jnp.zeros((N_COND - 1 - N_SAMPLE, D_MODEL), F32)], axis=0)
    cidx = _cond_index()
    states = []
    for l in range(DEPTH):
        mod = _modulation(cond, mod_w[l], mod_b[l])
        sh1, sc1, g1, sh2, sc2, g2 = jnp.split(mod, 6, axis=-1)
        h = _norm_mod(x, norm_g[l, 0], sc1, sh1, cidx, out_dtype=BF16)
        i = l // 2
        if l % 2 == 0:
            proj = _mm2(h, ab_w_in[i], tm=2048, tn=256)
            pp, ps = _split_runs(proj)
            prm = dict(mu=ab_mu[i], w_up=ab_w_up[i], w0=ab_w0[i], a_up=ab_a_up[i], a0=ab_a0[i], g_up=ab_g_up[i],
                       k_k=ab_k_k[i], k_a=ab_k_a[i], r_k=ab_r_k[i].reshape(W_A).reshape(N_HEADS, HEAD),
                       lnx_g=ab_lnx_g[i], lnx_b=ab_lnx_b[i], dw_w=ab_dw_w[i], dw_b=ab_dw_b[i],
                       ln_g=ab_ln_g[i], ln_b=ab_ln_b[i])
            y_a, s_fin = _rwkv_mix(pp[..., :RWKV_COLS], ps[..., :RWKV_COLS], state_rwkv[:, i], prm)
            states.append(s_fin)
            cv = jnp.concatenate([_conformer(q[..., RWKV_COLS:RWKV_COLS + W_B], q[..., RWKV_COLS + W_B:], prm)
                                  for q in (pp, ps)], axis=0)
            y = _mm2(jnp.concatenate([y_a, cv], axis=-1), ab_w_out[i], tm=2048, tn=256)
        else:
            u = _mm2(h, hy_w_in[i], tm=2048, tn=256)
            up, us = _split_runs(u)
            prm = dict(sc_w=hy_sc_w[i], sc_b=hy_sc_b[i], f_w1=hy_f_w1[i], f_b1=hy_f_b1[i], f_w2=hy_f_w2[i],
                       f_b2=hy_f_b2[i], f_w3=hy_f_w3[i], f_b3=hy_f_b3[i], f_w4=hy_f_w4[i], f_freq=hy_f_freq[i],
                       bias=hy_bias[i])
            y = _mm2(_hyena(up, us, prm), hy_w_out[i], tm=2048, tn=256)
        x = x + _expand_rows(g1) * y
        h2, logits = _norm_mod(x, norm_g[l, 1], sc2, sh2, cidx, out_dtype=BF16, router=moe_router[l])
        x = x + _expand_rows(g2) * _ec_moe(h2, logits, moe_w_gate[l], moe_w_up[l], moe_w_down[l])
    zeros = jnp.zeros((N_COND, D_MODEL), F32)
    y = _norm_mod(x, final_g, zeros, zeros, cidx, out_dtype=F32)
    y_prompt, y_sample = _split_runs(y)
    new_state = jnp.stack(states, axis=1).astype(x_prompt.dtype)
    return (y_prompt, y_sample, new_state)
```

```python
import functools
import math

import jax
import jax.numpy as jnp
from jax import lax
from jax.experimental import pallas as pl
from jax.experimental.pallas import tpu as pltpu

F32 = jnp.float32
BF16 = jnp.bfloat16

D_MODEL = 4096
N_PROMPT, T_PROMPT = 16, 256
N_SAMPLE, T_SAMPLE = 4, 1024
GRID_W = 64
DEPTH = 4
W_A = D_MODEL // 2
HEAD = 64
N_HEADS = W_A // HEAD
LORA_W = 128
LORA_A = 128
LORA_G = 256
RWKV_COLS = 3 * W_A + 2 * LORA_W + 2 * LORA_A + LORA_G
W_B = D_MODEL // 2
CONV_B = 31
HY_SHORT = 3
HY_EMB_DIM = 33
HY_TARGET = 1e-2
HY_MAX_DECAY = math.log(HY_TARGET) / 0.3
HY_MIN_DECAY = math.log(HY_TARGET) / 1.5
N_EXPERTS = 16
D_EXPERT = 1536
EC_FACTOR = 2
RMS_EPS = 1e-6
LN_EPS = 1e-5
GN_EPS = 64e-5

ROWS_P = N_PROMPT * T_PROMPT
ROWS_S = N_SAMPLE * T_SAMPLE
N_TOK = ROWS_P + ROWS_S
ROW_BLOCK = 256
N_COND = 8
LANES = 128
SUB = 8
KH = W_A // LANES
KL = LANES // N_HEADS
VMEM_LIMIT = 56 * 1024 * 1024

RUNS = ((0, N_PROMPT, T_PROMPT), (ROWS_P, N_SAMPLE, T_SAMPLE))


def _cparams(*sem):
    return pltpu.CompilerParams(dimension_semantics=sem, vmem_limit_bytes=VMEM_LIMIT)


def _sigmoid(x):
    return 1.0 / (1.0 + jnp.exp(-x))


def _mm_kernel(a_ref, w_ref, o_ref):
    o_ref[...] = jnp.dot(a_ref[...].astype(BF16), w_ref[...].astype(BF16),
                         preferred_element_type=F32).astype(o_ref.dtype)


def _mm(a, w, *, tm, tn, out_dtype=F32):
    m, k = a.shape
    n = w.shape[1]
    tm = min(tm, m)
    assert w.shape[0] == k and m % tm == 0 and n % tn == 0
    return pl.pallas_call(
        _mm_kernel,
        grid=(m // tm, n // tn),
        in_specs=[pl.BlockSpec((tm, k), lambda i, j: (i, 0)), pl.BlockSpec((k, tn), lambda i, j: (0, j))],
        out_specs=pl.BlockSpec((tm, tn), lambda i, j: (i, j)),
        out_shape=jax.ShapeDtypeStruct((m, n), out_dtype),
        compiler_params=_cparams("parallel", "arbitrary"),
        name="matmul",
    )(a, w)


def _glu_mm_kernel(a_ref, wa_ref, wb_ref, o_ref):
    a = a_ref[...]
    pa = jnp.dot(a, wa_ref[...].astype(BF16), preferred_element_type=F32)
    pb = jnp.dot(a, wb_ref[...].astype(BF16), preferred_element_type=F32)
    o_ref[...] = pa * _sigmoid(pb)


def _glu_mm(a, w, *, col_a, col_b, n, tm, tn):
    m, k = a.shape
    tm = min(tm, m)
    oa, ob = col_a // tn, col_b // tn
    assert col_a % tn == 0 and col_b % tn == 0
    return pl.pallas_call(
        _glu_mm_kernel,
        grid=(m // tm, n // tn),
        in_specs=[pl.BlockSpec((tm, k), lambda i, j: (i, 0)),
                  pl.BlockSpec((k, tn), lambda i, j: (0, oa + j)),
                  pl.BlockSpec((k, tn), lambda i, j: (0, ob + j))],
        out_specs=pl.BlockSpec((tm, tn), lambda i, j: (i, j)),
        out_shape=jax.ShapeDtypeStruct((m, n), F32),
        compiler_params=_cparams("parallel", "arbitrary"),
        name="glu_matmul",
    )(a, w, w)


def _mod_kernel(c_ref, w_ref, b_ref, o_ref):
    c = c_ref[...]
    o_ref[...] = jnp.dot(c * _sigmoid(c), w_ref[...], precision=lax.Precision.HIGHEST,
                         preferred_element_type=F32) + b_ref[...]


def _modulation(cond, w, b, *, tn=512):
    n = w.shape[1]
    return pl.pallas_call(
        _mod_kernel,
        grid=(n // tn,),
        in_specs=[pl.BlockSpec((N_COND, D_MODEL), lambda j: (0, 0)),
                  pl.BlockSpec((D_MODEL, tn), lambda j: (0, j)),
                  pl.BlockSpec((1, tn), lambda j: (0, j))],
        out_specs=pl.BlockSpec((N_COND, tn), lambda j: (0, j)),
        out_shape=jax.ShapeDtypeStruct((N_COND, n), F32),
        compiler_params=_cparams("arbitrary"),
        name="modulation",
    )(cond, w, b[None])


def _row_pattern(m):
    return jnp.stack([jnp.broadcast_to(m[0], (SUB, D_MODEL)), jnp.tile(m[1:1 + N_SAMPLE], (SUB // N_SAMPLE, 1))])


def _expand_rows(pat):
    return jnp.broadcast_to(pat[:, None], (2, ROWS_P // SUB, SUB, D_MODEL)).reshape(N_TOK, D_MODEL)


def _normed(x_ref, g_ref, sc_ref, sh_ref):
    x = x_ref[...]
    y = x * lax.rsqrt(jnp.mean(x * x, axis=-1, keepdims=True) + RMS_EPS) * g_ref[...]
    y = y.reshape(ROW_BLOCK // SUB, SUB, D_MODEL) * (1.0 + sc_ref[0]) + sh_ref[0]
    return y.reshape(ROW_BLOCK, D_MODEL)


def _norm_kernel(x_ref, g_ref, sc_ref, sh_ref, o_ref):
    o_ref[...] = _normed(x_ref, g_ref, sc_ref, sh_ref).astype(o_ref.dtype)


def _norm_router_kernel(x_ref, g_ref, sc_ref, sh_ref, r_ref, o_ref, l_ref):
    h = _normed(x_ref, g_ref, sc_ref, sh_ref)
    o_ref[...] = h.astype(o_ref.dtype)
    l_ref[...] = jnp.dot(h, r_ref[...], precision=lax.Precision.HIGHEST, preferred_element_type=F32)


def _norm_mod(x, g, sc_pat, sh_pat, *, out_dtype, router=None):
    nblk = N_TOK // ROW_BLOCK
    row = pl.BlockSpec((ROW_BLOCK, D_MODEL), lambda i: (i, 0))
    vec = pl.BlockSpec((1, D_MODEL), lambda i: (0, 0))
    pat = pl.BlockSpec((1, SUB, D_MODEL), lambda i: (i // (ROWS_P // ROW_BLOCK), 0, 0))
    in_specs = [row, vec, pat, pat]
    args = [x, g[None], sc_pat, sh_pat]
    out_specs = row
    out_shape = jax.ShapeDtypeStruct((N_TOK, D_MODEL), out_dtype)
    body = _norm_kernel
    if router is not None:
        in_specs.append(pl.BlockSpec((D_MODEL, N_EXPERTS), lambda i: (0, 0)))
        args.append(router)
        out_specs = [row, pl.BlockSpec((ROW_BLOCK, N_EXPERTS), lambda i: (i, 0))]
        out_shape = [out_shape, jax.ShapeDtypeStruct((N_TOK, N_EXPERTS), F32)]
        body = _norm_router_kernel
    return pl.pallas_call(
        body, grid=(nblk,), in_specs=in_specs, out_specs=out_specs, out_shape=out_shape,
        compiler_params=_cparams("arbitrary"), name="norm_mod",
    )(*args)


def _group_sum(x):
    x = x + pltpu.roll(x, N_HEADS, 1)
    return x + pltpu.roll(x, 2 * N_HEADS, 1)


def _head_sum(x):
    acc = x[:, :LANES]
    for j in range(1, KH):
        acc = acc + x[:, j * LANES:(j + 1) * LANES]
    return _group_sum(acc)


def _prep_kernel(m_ref, wup_ref, w0_ref, aup_ref, a0_ref, gup_ref, kscale_ref, ka_ref, rk_ref,
                 kk_out, vrep_out, w_out, wr_out, b_out, kd_out, br_out, kr_out, g_out, bonus_out):
    r = m_ref[:, :W_A]
    k = m_ref[:, W_A:2 * W_A]
    v = m_ref[:, 2 * W_A:3 * W_A]
    o = 3 * W_A
    wd = jnp.tanh(m_ref[:, o:o + 2 * LORA_W]).astype(BF16)
    o += 2 * LORA_W
    ad = m_ref[:, o:o + 2 * LORA_A].astype(BF16)
    o += 2 * LORA_A
    gd = _sigmoid(m_ref[:, o:o + LORA_G]).astype(BF16)

    kx = k * kscale_ref[...]
    inv = jnp.minimum(lax.rsqrt(_head_sum(kx * kx)), 1e12)
    kk = kx * jnp.tile(inv, (1, KH))
    kk_out[...] = kk
    grp = lax.broadcasted_iota(jnp.int32, (v.shape[0], LANES), 1) // N_HEADS
    for j in range(KH):
        vj = v[:, j * LANES:(j + 1) * LANES]
        for g in range(KL):
            i = j * KL + g
            rep = _group_sum(jnp.where(grp == g, vj, 0.0))
            for rg in range(rep.shape[0] // SUB):
                vrep_out[rg, i] = rep[rg * SUB:(rg + 1) * SUB]
    g_out[...] = jnp.dot(gd, gup_ref[...].astype(BF16), preferred_element_type=F32)
    bonus = None
    for z in range(2):
        wl = w0_ref[z] + jnp.dot(wd[:, z * LORA_W:(z + 1) * LORA_W], wup_ref[z].astype(BF16),
                                 preferred_element_type=F32)
        neg = -wl
        softplus = jnp.maximum(neg, 0.0) + jnp.log(1.0 + jnp.exp(-jnp.abs(neg)))
        w = jnp.exp(-jnp.exp(-softplus - 0.5))
        a = _sigmoid(a0_ref[z] + jnp.dot(ad[:, z * LORA_A:(z + 1) * LORA_A], aup_ref[z].astype(BF16),
                                         preferred_element_type=F32))
        kd = k * (1.0 + (a - 1.0) * ka_ref[...])
        b = kk * a
        w_out[z] = w
        wr_out[z] = w * r
        b_out[z] = b
        kd_out[z] = kd
        br_out[z] = _head_sum(b * r)
        kr_out[z] = _head_sum(kd * r)
        hs = jnp.tile(_head_sum(r * kd * rk_ref[...]), (1, KH))
        bonus = hs if bonus is None else bonus + hs
    bonus_out[...] = bonus * v


def _rwkv_prep(mixed, prm, *, rb=128):
    rows = mixed.shape[0]
    full = lambda shape: pl.BlockSpec(shape, lambda i: (0,) * len(shape))
    one = pl.BlockSpec((rb, W_A), lambda i: (i, 0))
    two = pl.BlockSpec((2, rb, W_A), lambda i: (0, i, 0))
    dot2 = pl.BlockSpec((2, rb, LANES), lambda i: (0, i, 0))
    s1 = jax.ShapeDtypeStruct((rows, W_A), F32)
    s2 = jax.ShapeDtypeStruct((2, rows, W_A), F32)
    sd = jax.ShapeDtypeStruct((2, rows, LANES), F32)
    return pl.pallas_call(
        _prep_kernel,
        grid=(rows // rb,),
        in_specs=[pl.BlockSpec((rb, RWKV_COLS), lambda i: (i, 0)),
                  full((2, LORA_W, W_A)), full((2, 1, W_A)), full((2, LORA_A, W_A)), full((2, 1, W_A)),
                  full((LORA_G, W_A)), full((1, W_A)), full((1, W_A)), full((1, W_A))],
        out_specs=[one, pl.BlockSpec((rb // SUB, HEAD, SUB, LANES), lambda i: (i, 0, 0, 0)), two, two, two, two,
                   dot2, dot2, one, one],
        out_shape=[s1, jax.ShapeDtypeStruct((rows // SUB, HEAD, SUB, LANES), F32), s2, s2, s2, s2, sd, sd, s1, s1],
        compiler_params=_cparams("parallel"),
        name="rwkv_prep",
    )(mixed, prm["w_up"], prm["w0"][:, None, :], prm["a_up"], prm["a0"][:, None, :], prm["g_up"],
      prm["k_k"][None], prm["k_a"][None], prm["r_k"][None])


def _scan_step(op, vrep, br, kr, s_scr, sa_scr, y_scr, raw_scr, unroll):
    grp = lax.broadcasted_iota(jnp.int32, (SUB, LANES), 1) // N_HEADS

    def accumulate(ig):
        p1 = [None] * KL
        p2 = [None] * KL
        for j in range(KH):
            kkj = op(0, j)
            wrj = op(1, j)
            for g in range(KL):
                s = s_scr[ig * KL + g, j]
                a1 = s * kkj
                a2 = s * wrj
                p1[g] = a1 if p1[g] is None else p1[g] + a1
                p2[g] = a2 if p2[g] is None else p2[g] + a2
        for g in range(KL):
            raw_scr[0, ig * KL + g] = p1[g]
            raw_scr[1, ig * KL + g] = p2[g]

    def fold_once(ig):
        for g in range(KL):
            for q in range(2):
                x = raw_scr[q, ig * KL + g]
                raw_scr[q, ig * KL + g] = x + pltpu.roll(x, N_HEADS, 1)

    def fold_twice(ig):
        yv = None
        for g in range(KL):
            i = ig * KL + g
            x1 = raw_scr[0, i]
            x2 = raw_scr[1, i]
            sa = x1 + pltpu.roll(x1, 2 * N_HEADS, 1)
            sa_scr[i] = sa
            y_g = x2 + pltpu.roll(x2, 2 * N_HEADS, 1) - sa * br + vrep(i) * kr
            yv = y_g if yv is None else jnp.where(grp == g, y_g, yv)
        y_scr[ig] = yv

    lag = 2
    for n in range(KH + 2 * lag):
        if 2 * lag <= n:
            fold_twice(n - 2 * lag)
        if lag <= n < KH + lag:
            fold_once(n - lag)
        if n < KH:
            accumulate(n)

    def update_group(ig, c):
        sa = [sa_scr[ig * KL + g] for g in range(KL)]
        vi = [vrep(ig * KL + g) for g in range(KL)]
        for j in range(KH):
            wj = op(2, j)
            bj = op(3, j)
            kdj = op(4, j)
            for g in range(KL):
                idx = ig * KL + g
                s_scr[idx, j] = s_scr[idx, j] * wj + (vi[g] * kdj - sa[g] * bj)
        return c

    lax.fori_loop(0, KH, update_group, 0, unroll=unroll)


_SCAN_SCRATCH = [pltpu.VMEM((HEAD, KH, SUB, LANES), F32), pltpu.VMEM((HEAD, SUB, LANES), F32),
                 pltpu.VMEM((KH, SUB, LANES), F32), pltpu.VMEM((2, HEAD, SUB, LANES), F32)]


def _scan_prompt_kernel(kk_ref, vrep_ref, w_ref, wr_ref, b_ref, kd_ref, br_ref, kr_ref, y_ref, sfin_ref,
                        s_scr, sa_scr, y_scr, raw_scr, *, tb, nt, unroll):
    d = pl.program_id(0)
    tblk = pl.program_id(2)

    @pl.when(tblk == 0)
    def _():
        s_scr[...] = jnp.zeros(s_scr.shape, F32)

    def step(tt, c):
        t = tt + d * (tb - 1 - 2 * tt)
        refs = (kk_ref.at[t, 0], wr_ref.at[0, t, 0], w_ref.at[0, t, 0], b_ref.at[0, t, 0], kd_ref.at[0, t, 0])
        op = lambda slot, j: refs[slot][:, j * LANES:(j + 1) * LANES]
        vrep = lambda i: vrep_ref[t, 0, i]
        _scan_step(op, vrep, br_ref[0, t, 0], kr_ref[0, t, 0], s_scr, sa_scr, y_scr, raw_scr, unroll)
        for j in range(KH):
            y_ref[0, t, 0, :, j * LANES:(j + 1) * LANES] = y_scr[j]
        return c

    lax.fori_loop(0, tb, step, 0)

    @pl.when(tblk == nt - 1)
    def _():
        sfin_ref[0, 0] = s_scr[...]


def _rwkv_scan_prompt(kk, vrep, w, wr, b, kd, br, kr, *, tb=16, unroll=2):
    t, nb = kk.shape[0], kk.shape[1]
    nt = t // tb

    def tmap(d, j):
        return j + d * (nt - 1 - 2 * j)

    shared = pl.BlockSpec((tb, 1, SUB, W_A), lambda d, bi, j: (tmap(d, j), bi, 0, 0))
    values = pl.BlockSpec((tb, 1, HEAD, SUB, LANES), lambda d, bi, j: (tmap(d, j), bi, 0, 0, 0))
    perdir = pl.BlockSpec((1, tb, 1, SUB, W_A), lambda d, bi, j: (d, tmap(d, j), bi, 0, 0))
    dots = pl.BlockSpec((1, tb, 1, SUB, LANES), lambda d, bi, j: (d, tmap(d, j), bi, 0, 0))
    state = pl.BlockSpec((1, 1, HEAD, KH, SUB, LANES), lambda d, bi, j: (d, bi, 0, 0, 0, 0))
    return pl.pallas_call(
        functools.partial(_scan_prompt_kernel, tb=tb, nt=nt, unroll=unroll),
        grid=(2, nb, nt),
        in_specs=[shared, values, perdir, perdir, perdir, perdir, dots, dots],
        out_specs=[perdir, state],
        out_shape=[jax.ShapeDtypeStruct((2, t, nb, SUB, W_A), F32),
                   jax.ShapeDtypeStruct((2, nb, HEAD, KH, SUB, LANES), F32)],
        scratch_shapes=_SCAN_SCRATCH,
        compiler_params=_cparams("parallel", "parallel", "arbitrary"),
        name="rwkv_scan_prompt",
    )(kk, vrep, w, wr, b, kd, br, kr)


def _scan_sample_kernel(kkf, kkb, vf, vb, wf, wb, wrf, wrb, bf, bb, kdf, kdb, brf, brb, krf, krb, s0_ref,
                        yf_ref, yb_ref, s_scr, sa_scr, y_scr, raw_scr, ops_scr, v_scr, *, tb, unroll):
    tblk = pl.program_id(0)
    half = SUB // 2

    @pl.when(tblk == 0)
    def _():
        s_scr[...] = s0_ref[...]

    def merge(a8, b8, odd):
        low = lax.broadcasted_iota(jnp.int32, a8.shape, 0) < half
        if odd:
            return pltpu.roll(jnp.where(low, b8, a8), half, 0)
        return jnp.where(low, a8, b8)

    def two_steps(t2, c):
        for odd in (False, True):
            tt = 2 * t2 + (1 if odd else 0)
            gf = t2
            gb = (tb - 1 - tt) // 2
            pairs = ((kkf.at[gf], kkb.at[gb]), (wrf.at[0, gf], wrb.at[0, gb]), (wf.at[0, gf], wb.at[0, gb]),
                     (bf.at[0, gf], bb.at[0, gb]), (kdf.at[0, gf], kdb.at[0, gb]))
            for slot, (fr, br_) in enumerate(pairs):
                for j in range(KH):
                    sl = slice(j * LANES, (j + 1) * LANES)
                    ops_scr[slot, j] = merge(fr[:, sl], br_[:, sl], odd)

            def stage_values(i, c2):
                v_scr[i] = merge(vf[gf, i], vb[gb, i], odd)
                return c2

            lax.fori_loop(0, HEAD, stage_values, 0, unroll=8)
            br = merge(brf[0, gf], brb[0, gb], odd)
            kr = merge(krf[0, gf], krb[0, gb], odd)
            _scan_step(lambda slot, j: ops_scr[slot, j], lambda i: v_scr[i], br, kr, s_scr, sa_scr, y_scr, raw_scr,
                       unroll)
            for j in range(KH):
                sl = slice(j * LANES, (j + 1) * LANES)
                y8 = y_scr[j]
                if odd:
                    y8 = pltpu.roll(y8, half, 0)
                    yf_ref[gf, half:, sl] = y8[half:]
                    yb_ref[gb, :half, sl] = y8[:half]
                else:
                    yf_ref[gf, :half, sl] = y8[:half]
                    yb_ref[gb, half:, sl] = y8[half:]
        return c

    lax.fori_loop(0, tb // 2, two_steps, 0)


def _rwkv_scan_sample(kk, vrep, w, wr, b, kd, br, kr, s0, *, tb=16, unroll=2):
    t2 = kk.shape[0]
    nt = 2 * t2 // tb
    g = tb // 2
    fw = lambda j: j
    bw = lambda j: nt - 1 - j
    sh = lambda m: pl.BlockSpec((g, SUB, W_A), lambda j: (m(j), 0, 0))
    val = lambda m: pl.BlockSpec((g, HEAD, SUB, LANES), lambda j: (m(j), 0, 0, 0))
    pd = lambda d, m: pl.BlockSpec((1, g, SUB, W_A), lambda j: (d, m(j), 0, 0))
    dt = lambda d, m: pl.BlockSpec((1, g, SUB, LANES), lambda j: (d, m(j), 0, 0))
    st = pl.BlockSpec((HEAD, KH, SUB, LANES), lambda j: (0, 0, 0, 0))
    return pl.pallas_call(
        functools.partial(_scan_sample_kernel, tb=tb, unroll=unroll),
        grid=(nt,),
        in_specs=[sh(fw), sh(bw), val(fw), val(bw), pd(0, fw), pd(1, bw), pd(0, fw), pd(1, bw), pd(0, fw),
                  pd(1, bw), pd(0, fw), pd(1, bw), dt(0, fw), dt(1, bw), dt(0, fw), dt(1, bw), st],
        out_specs=[sh(fw), sh(bw)],
        out_shape=[jax.ShapeDtypeStruct((t2, SUB, W_A), F32), jax.ShapeDtypeStruct((t2, SUB, W_A), F32)],
        scratch_shapes=_SCAN_SCRATCH + [pltpu.VMEM((5, KH, SUB, LANES), F32), pltpu.VMEM((HEAD, SUB, LANES), F32)],
        compiler_params=_cparams("arbitrary"),
        name="rwkv_scan_sample",
    )(kk, kk, vrep, vrep, w, w, wr, wr, b, b, kd, kd, br, br, kr, kr, s0)


def _post_kernel(y0_ref, y1_ref, bonus_ref, g_ref, lg_ref, lb_ref, o_ref):
    y = y0_ref[...] + y1_ref[...]
    mu = jnp.tile(_head_sum(y) * (1.0 / HEAD), (1, KH))
    c = y - mu
    var = jnp.tile(_head_sum(c * c) * (1.0 / HEAD), (1, KH))
    yn = c * lax.rsqrt(var + GN_EPS) * lg_ref[...] + lb_ref[...]
    o_ref[...] = ((yn + bonus_ref[...]) * g_ref[...]).astype(o_ref.dtype)


def _rwkv_post(y0, y1, bonus, g, lnx_g, lnx_b, *, rb=256):
    rows = y0.shape[0]
    one = pl.BlockSpec((rb, W_A), lambda i: (i, 0))
    vec = pl.BlockSpec((1, W_A), lambda i: (0, 0))
    return pl.pallas_call(
        _post_kernel, grid=(rows // rb,), in_specs=[one, one, one, one, vec, vec], out_specs=one,
        out_shape=jax.ShapeDtypeStruct((rows, W_A), BF16),
        compiler_params=_cparams("parallel"), name="rwkv_post",
    )(y0, y1, bonus, g, lnx_g[None], lnx_b[None])


CONF_CHUNK = 512
CONF_NCH = W_B // CONF_CHUNK


def _conf_kernel(prev_ref, cur_ref, next_ref, w_ref, b_ref, lg_ref, lb_ref, o_ref, ext_scr, acc_scr,
                 *, nb, blocks):
    i = pl.program_id(0)
    rb = cur_ref.shape[0]
    has_prev = (i > 0).astype(F32)
    has_next = (i < blocks - 1).astype(F32)
    for c in range(CONF_NCH):
        cs = slice(c * CONF_CHUNK, (c + 1) * CONF_CHUNK)
        ext_scr[c, :rb] = prev_ref[:, cs] * has_prev
        ext_scr[c, rb:2 * rb] = cur_ref[:, cs]
        ext_scr[c, 2 * rb:] = next_ref[:, cs] * has_next
    pad = (CONV_B - 1) // 2

    def chunk(c, sums):
        acc = None
        for j in range(CONV_B):
            lo = rb + (j - pad) * nb
            term = ext_scr[c, lo:lo + rb] * w_ref[c, j:j + 1]
            acc = term if acc is None else acc + term
        acc = acc + b_ref[c]
        acc_scr[c] = acc
        return sums + jnp.sum(acc, axis=-1, keepdims=True)

    mu = lax.fori_loop(0, CONF_NCH, chunk, jnp.zeros((rb, 1), F32)) * (1.0 / W_B)
    var = jnp.zeros((rb, 1), F32)
    for c in range(CONF_NCH):
        d = acc_scr[c] - mu
        var = var + jnp.sum(d * d, axis=-1, keepdims=True)
    scale = lax.rsqrt(var * (1.0 / W_B) + LN_EPS)
    for c in range(CONF_NCH):
        y = (acc_scr[c] - mu) * scale * lg_ref[c] + lb_ref[c]
        o_ref[:, c * CONF_CHUNK:(c + 1) * CONF_CHUNK] = (y * _sigmoid(y)).astype(o_ref.dtype)


def _conformer(glu, prm, *, rb=256):
    chunked = lambda a: jnp.swapaxes(a.reshape(a.shape[0], CONF_NCH, CONF_CHUNK), 0, 1)
    outs = []
    for lo, nb, t in RUNS:
        rows = nb * t
        blocks = rows // rb
        base = lo // rb
        assert (CONV_B - 1) // 2 * nb <= rb
        spec = lambda f: pl.BlockSpec((rb, W_B), f)
        vec = lambda n: pl.BlockSpec((CONF_NCH, n, CONF_CHUNK), lambda i: (0, 0, 0))
        outs.append(pl.pallas_call(
            functools.partial(_conf_kernel, nb=nb, blocks=blocks),
            grid=(blocks,),
            in_specs=[spec(lambda i: (base + jnp.maximum(i - 1, 0), 0)), spec(lambda i: (base + i, 0)),
                      spec(lambda i: (base + jnp.minimum(i + 1, blocks - 1), 0)),
                      vec(CONV_B), vec(1), vec(1), vec(1)],
            out_specs=pl.BlockSpec((rb, W_B), lambda i: (i, 0)),
            out_shape=jax.ShapeDtypeStruct((rows, W_B), BF16),
            scratch_shapes=[pltpu.VMEM((CONF_NCH, 3 * rb, CONF_CHUNK), F32),
                            pltpu.VMEM((CONF_NCH, rb, CONF_CHUNK), F32)],
            compiler_params=_cparams("parallel"),
            name="conformer_conv",
        )(glu, glu, glu, chunked(prm["dw_w"]), chunked(prm["dw_b"][None]), chunked(prm["ln_g"][None]),
          chunked(prm["ln_b"][None])))
    return jnp.concatenate(outs, axis=0)


def _hyena_kernel(x0_ref, x1_ref, v_ref, w0_ref, w1_ref, wv_ref, b0_ref, b1_ref, bv_ref, bias_ref, kf_ref,
                  fwd_ref, inv_ref, o_ref):
    length = x0_ref.shape[0]
    row = lax.broadcasted_iota(jnp.int32, x0_ref.shape, 0)

    def short(u_ref, w_ref, b_ref):
        u = u_ref[...]
        prev = jnp.where(row == 0, 0.0, pltpu.roll(u, 1, 0))
        nxt = jnp.where(row == length - 1, 0.0, pltpu.roll(u, length - 1, 0))
        return prev * w_ref[0:1] + u * w_ref[1:2] + nxt * w_ref[2:3] + b_ref[...]

    x1 = short(x1_ref, w1_ref, b1_ref)
    v = short(v_ref, wv_ref, bv_ref)
    z = v * x1
    zf = jnp.dot(fwd_ref[...], z.astype(BF16), preferred_element_type=F32)
    zc, zs = zf[:length], zf[length:]
    kc, ks = kf_ref[:length], kf_ref[length:]
    first = row == 0
    yc = jnp.where(first, zc * kc, zc * kc - zs * ks)
    ys = jnp.where(first, zs * ks, zc * ks + zs * kc)
    yf = jnp.concatenate([yc, ys], axis=0).astype(BF16)
    y = jnp.dot(inv_ref[...], yf, preferred_element_type=F32)
    x0 = short(x0_ref, w0_ref, b0_ref)
    o_ref[...] = ((y + bias_ref[...] * z) * x0).astype(o_ref.dtype)


def _dft_tables(length):
    n = 2 * length
    kidx = jnp.arange(length, dtype=jnp.int32)[:, None]
    sidx = jnp.arange(n, dtype=jnp.int32)[None, :]
    ang = (2.0 * math.pi / n) * ((kidx * sidx) % n).astype(F32)
    cos = jnp.cos(ang)
    sin = jnp.where(kidx == 0, jnp.where(sidx % 2 == 0, 1.0, -1.0).astype(F32), jnp.sin(ang))
    fwd = jnp.concatenate([cos, sin], axis=0)
    scale = jnp.where(kidx == 0, 1.0 / n, 2.0 / n)
    inv = jnp.concatenate([(cos * scale)[:, :length].T, (sin * scale)[:, :length].T], axis=1)
    return fwd, inv


def _hyena_filters(length, prm):
    hi = lax.Precision.HIGHEST
    t = jnp.linspace(0.0, 1.0, length, dtype=F32)[:, None]
    bands = (HY_EMB_DIM - 1) // 2
    ang = (2.0 * math.pi / length) * jnp.arange(length, dtype=F32)[:, None] \
        * jnp.linspace(1e-4, bands - 1, bands, dtype=F32)[None, :]
    z = jnp.concatenate([t, jnp.cos(ang), -jnp.sin(ang)], axis=-1)
    fr = prm["f_freq"]
    h = jnp.sin(fr * (jnp.dot(z, prm["f_w1"], precision=hi) + prm["f_b1"]))
    h = jnp.sin(fr * (jnp.dot(h, prm["f_w2"], precision=hi) + prm["f_b2"]))
    h = jnp.sin(fr * (jnp.dot(h, prm["f_w3"], precision=hi) + prm["f_b3"]))
    h = jnp.dot(h, prm["f_w4"], precision=hi).reshape(length, 2, D_MODEL)
    deltas = jnp.abs(jnp.linspace(HY_MIN_DECAY, HY_MAX_DECAY, D_MODEL, dtype=F32))
    h = h * jnp.exp(-t * deltas)[:, None, :]
    h = h / (jnp.sum(jnp.abs(h), axis=(0, 1)) + 1e-6)
    return h[:, 0], h[:, 1]


def _hyena(u, prm):
    outs = []
    for lo, nb, t in RUNS:
        tn = 1024 if t <= 256 else 256
        per = D_MODEL // tn
        hf, hb = _hyena_filters(t, prm)
        fwd, inv = _dft_tables(t)
        kern = jnp.concatenate([hf, jnp.zeros((1, D_MODEL), F32), jnp.flip(hb[1:], axis=0)], axis=0)
        kf = _mm(fwd, kern, tm=min(2 * t, 1024), tn=1024)
        uv = u.reshape(N_TOK // nb, nb * 3 * D_MODEL)
        rblk = lo // (nb * t)
        part = lambda p: pl.BlockSpec((t, tn), lambda b, j: (rblk, b * 3 * per + p * per + j))
        wpart = lambda p, n: pl.BlockSpec((n, tn), lambda b, j: (0, p * per + j))
        chan = lambda n: pl.BlockSpec((n, tn), lambda b, j: (0, j))
        const = lambda s: pl.BlockSpec(s, lambda b, j: (0, 0))
        out = pl.pallas_call(
            _hyena_kernel,
            grid=(nb, per),
            in_specs=[part(0), part(1), part(2), wpart(0, HY_SHORT), wpart(1, HY_SHORT), wpart(2, HY_SHORT),
                      wpart(0, 1), wpart(1, 1), wpart(2, 1), chan(1), chan(2 * t),
                      const((2 * t, t)), const((t, 2 * t))],
            out_specs=pl.BlockSpec((t, tn), lambda b, j: (0, b * per + j)),
            out_shape=jax.ShapeDtypeStruct((t, nb * D_MODEL), BF16),
            compiler_params=_cparams("parallel", "arbitrary"),
            name="hyena_conv",
        )(uv, uv, uv, prm["sc_w"], prm["sc_w"], prm["sc_w"], prm["sc_b"][None], prm["sc_b"][None],
          prm["sc_b"][None], prm["bias"][None], kf, fwd[:, :t].astype(BF16), inv.astype(BF16))
        outs.append(out.reshape(nb * t, D_MODEL))
    return jnp.concatenate(outs, axis=0)


def _moe_up_kernel(x_ref, wg_ref, wu_ref, h_ref):
    x = x_ref[0]
    g = jnp.dot(x, wg_ref[0].astype(BF16), preferred_element_type=F32)
    u = jnp.dot(x, wu_ref[0].astype(BF16), preferred_element_type=F32)
    h_ref[0] = (g * _sigmoid(g) * u).astype(h_ref.dtype)


def _moe_down_kernel(h_ref, wd_ref, gate_ref, o_ref):
    o_ref[0] = jnp.dot(h_ref[0], wd_ref[0].astype(BF16), preferred_element_type=F32) * gate_ref[0]


def _moe_experts(xg, gate, wg, wu, wd, *, tf=256, tn=512):
    e, rows, _ = xg.shape
    h = pl.pallas_call(
        _moe_up_kernel,
        grid=(e, D_EXPERT // tf),
        in_specs=[pl.BlockSpec((1, rows, D_MODEL), lambda ei, f: (ei, 0, 0)),
                  pl.BlockSpec((1, D_MODEL, tf), lambda ei, f: (ei, 0, f)),
                  pl.BlockSpec((1, D_MODEL, tf), lambda ei, f: (ei, 0, f))],
        out_specs=pl.BlockSpec((1, rows, tf), lambda ei, f: (ei, 0, f)),
        out_shape=jax.ShapeDtypeStruct((e, rows, D_EXPERT), BF16),
        compiler_params=_cparams("parallel", "arbitrary"),
        name="moe_up",
    )(xg, wg, wu)
    return pl.pallas_call(
        _moe_down_kernel,
        grid=(e, D_MODEL // tn),
        in_specs=[pl.BlockSpec((1, rows, D_EXPERT), lambda ei, n: (ei, 0, 0)),
                  pl.BlockSpec((1, D_EXPERT, tn), lambda ei, n: (ei, 0, n)),
                  pl.BlockSpec((1, rows, 1), lambda ei, n: (ei, 0, 0))],
        out_specs=pl.BlockSpec((1, rows, tn), lambda ei, n: (ei, 0, n)),
        out_shape=jax.ShapeDtypeStruct((e, rows, D_MODEL), F32),
        compiler_params=_cparams("parallel", "arbitrary"),
        name="moe_down",
    )(h, wd, gate)


def _ec_moe(h, logits, wg, wu, wd):
    aff = jax.nn.softmax(logits, axis=-1)
    gates, idxs = [], []
    for lo, nb, t in RUNS:
        cap = EC_FACTOR * t // N_EXPERTS
        a = jnp.transpose(aff[lo:lo + nb * t].reshape(t, nb, N_EXPERTS), (1, 2, 0))
        gate, idx = lax.top_k(a, cap)
        rows = lo + idx * nb + jnp.arange(nb, dtype=jnp.int32)[:, None, None]
        idxs.append(jnp.swapaxes(rows, 0, 1).reshape(N_EXPERTS, nb * cap))
        gates.append(jnp.swapaxes(gate, 0, 1).reshape(N_EXPERTS, nb * cap))
    idx = jnp.concatenate(idxs, axis=1)
    gate = jnp.concatenate(gates, axis=1)
    out = _moe_experts(h[idx], gate[..., None], wg, wu, wd)
    return jnp.zeros((N_TOK, D_MODEL), F32).at[idx.reshape(-1)].add(out.reshape(-1, D_MODEL))


def _key_major(x):
    return jnp.swapaxes(x.reshape(x.shape[:-1] + (N_HEADS, HEAD)), -1, -2).reshape(x.shape)


def _time_major(x):
    return jnp.swapaxes(x, 0, 1).reshape(x.shape[0] * x.shape[1], x.shape[2])


def _batch_major(x, nb, t):
    return jnp.swapaxes(x.reshape(t, nb, x.shape[-1]), 0, 1)


def _shift_steps(p, steps):
    if steps > 0:
        return jnp.pad(p[:-steps], ((steps, 0), (0, 0), (0, 0)))
    return jnp.pad(p[-steps:], ((0, -steps), (0, 0), (0, 0)))


def _token_shift_mix(p, quad, mu):
    outs = []
    for lo, nb, t in RUNS:
        q = p[lo:lo + nb * t].reshape(t, nb, RWKV_COLS)
        if t == T_PROMPT:
            shifted = jnp.where(quad % 2 == 0, _shift_steps(q, 1), _shift_steps(q, -1))
        else:
            col = (lax.broadcasted_iota(jnp.int32, (t, 1, 1), 0) % GRID_W)
            left = jnp.where(col == 0, 0.0, _shift_steps(q, 1))
            right = jnp.where(col == GRID_W - 1, 0.0, _shift_steps(q, -1))
            shifted = jnp.where(quad == 0, left, jnp.where(quad == 1, right, jnp.where(
                quad == 2, _shift_steps(q, GRID_W), _shift_steps(q, -GRID_W))))
        outs.append((q + (shifted - q) * mu).reshape(nb * t, RWKV_COLS))
    return outs


def _rwkv_mixer(mixed, s0_sample, prm):
    kk, vrep, w, wr, b, kd, br, kr, g, bonus = _rwkv_prep(mixed[0], prm)
    nbt = N_PROMPT // SUB
    tile = lambda x: x.reshape(x.shape[:-2] + (T_PROMPT, nbt, SUB, x.shape[-1]))
    y_p, s_p = _rwkv_scan_prompt(tile(kk), vrep.reshape(T_PROMPT, nbt, HEAD, SUB, LANES), tile(w), tile(wr),
                                 tile(b), tile(kd), tile(br), tile(kr))
    y_p = y_p.reshape(2, ROWS_P, W_A)
    y_ap = _rwkv_post(y_p[0], y_p[1], bonus, g, prm["lnx_g"], prm["lnx_b"])
    s_p = s_p.reshape(2, nbt, HEAD, KH, SUB, KL, N_HEADS)
    s_p = jnp.transpose(s_p, (1, 4, 0, 6, 2, 3, 5)).reshape(N_PROMPT, 2, N_HEADS, HEAD, HEAD)
    kk, vrep, w, wr, b, kd, br, kr, g, bonus = _rwkv_prep(mixed[1], prm)
    tile = lambda x: x.reshape(x.shape[:-2] + (T_SAMPLE // 2, SUB, x.shape[-1]))
    s0 = s0_sample.reshape(N_SAMPLE, 2, N_HEADS, HEAD, KH, KL)
    s0 = jnp.transpose(s0, (3, 4, 1, 0, 5, 2)).reshape(HEAD, KH, SUB, LANES)
    yf, yb = _rwkv_scan_sample(tile(kk), vrep, tile(w), tile(wr), tile(b), tile(kd), tile(br), tile(kr), s0)
    y_as = _rwkv_post(yf.reshape(ROWS_S, W_A), yb.reshape(ROWS_S, W_A), bonus, g, prm["lnx_g"], prm["lnx_b"])
    return jnp.concatenate([y_ap, y_as], axis=0), s_p


def kernel(x_prompt, x_sample, state_rwkv, c, c_ctx, norm_g, mod_w, mod_b, ab_w_in, ab_mu, ab_w_up, ab_w0, ab_a_up, ab_a0, ab_g_up, ab_k_k, ab_k_a, ab_r_k, ab_lnx_g, ab_lnx_b, ab_dw_w, ab_dw_b, ab_ln_g, ab_ln_b, ab_w_out, hy_w_in, hy_sc_w, hy_sc_b, hy_f_w1, hy_f_b1, hy_f_w2, hy_f_b2, hy_f_w3, hy_f_b3, hy_f_w4, hy_f_freq, hy_bias, hy_w_out, moe_router, moe_w_gate, moe_w_up, moe_w_down, final_g):
    x = jnp.concatenate([_time_major(x_prompt), _time_major(x_sample)], axis=0)
    cond = jnp.concatenate([c_ctx[None], c, jnp.zeros((N_COND - 1 - N_SAMPLE, D_MODEL), F32)], axis=0)
    chan = jnp.arange(RWKV_COLS, dtype=jnp.int32)
    quad = jnp.concatenate([_key_major(chan[:3 * W_A].reshape(3, W_A)).reshape(-1), chan[3 * W_A:]]) % 4
    states = []
    for l in range(DEPTH):
        mod = _modulation(cond, mod_w[l], mod_b[l])
        sh1, sc1, g1, sh2, sc2, g2 = [_row_pattern(m) for m in jnp.split(mod, 6, axis=-1)]
        h = _norm_mod(x, norm_g[l, 0], sc1, sh1, out_dtype=BF16)
        i = l // 2
        if l % 2 == 0:
            km3 = lambda a: jnp.concatenate([_key_major(a[..., :3 * W_A].reshape(a.shape[:-1] + (3, W_A))).reshape(
                a.shape[:-1] + (3 * W_A,)), a[..., 3 * W_A:]], axis=-1)
            w_rwkv = km3(ab_w_in[i][:, :RWKV_COLS])
            prm = dict(w_up=_key_major(ab_w_up[i]), w0=_key_major(ab_w0[i]), a_up=_key_major(ab_a_up[i]),
                       a0=_key_major(ab_a0[i]), g_up=_key_major(ab_g_up[i]), k_k=_key_major(ab_k_k[i]),
                       k_a=_key_major(ab_k_a[i]), r_k=_key_major(ab_r_k[i].reshape(W_A)),
                       lnx_g=_key_major(ab_lnx_g[i]), lnx_b=_key_major(ab_lnx_b[i]),
                       dw_w=ab_dw_w[i], dw_b=ab_dw_b[i], ln_g=ab_ln_g[i], ln_b=ab_ln_b[i])
            proj = _mm(h, w_rwkv, tm=2048, tn=256)
            mixed = _token_shift_mix(proj, quad, km3(ab_mu[i]))
            y_a, s_fin = _rwkv_mixer(mixed, state_rwkv[:, i], prm)
            states.append(s_fin)
            glu = _glu_mm(h, ab_w_in[i], col_a=RWKV_COLS, col_b=RWKV_COLS + W_B, n=W_B, tm=1024, tn=256)
            cv = _conformer(glu, prm)
            w_out = jnp.concatenate([jnp.swapaxes(ab_w_out[i][:W_A].reshape(N_HEADS, HEAD, D_MODEL), 0, 1).reshape(
                W_A, D_MODEL), ab_w_out[i][W_A:]], axis=0)
            y = _mm(jnp.concatenate([y_a, cv], axis=-1), w_out, tm=2048, tn=256)
        else:
            u = _mm(h, hy_w_in[i], tm=2048, tn=256)
            prm = dict(sc_w=hy_sc_w[i], sc_b=hy_sc_b[i], f_w1=hy_f_w1[i], f_b1=hy_f_b1[i], f_w2=hy_f_w2[i],
                       f_b2=hy_f_b2[i], f_w3=hy_f_w3[i], f_b3=hy_f_b3[i], f_w4=hy_f_w4[i], f_freq=hy_f_freq[i],
                       bias=hy_bias[i])
            y = _mm(_hyena(u, prm), hy_w_out[i], tm=2048, tn=256)
        x = x + _expand_rows(g1) * y
        h2, logits = _norm_mod(x, norm_g[l, 1], sc2, sh2, out_dtype=BF16, router=moe_router[l])
        x = x + _expand_rows(g2) * _ec_moe(h2, logits, moe_w_gate[l], moe_w_up[l], moe_w_down[l])
    zeros = jnp.zeros((2, SUB, D_MODEL), F32)
    y = _norm_mod(x, final_g, zeros, zeros, out_dtype=F32)
    y_prompt = _batch_major(y[:ROWS_P], N_PROMPT, T_PROMPT)
    y_sample = _batch_major(y[ROWS_P:], N_SAMPLE, T_SAMPLE)
    new_state = jnp.stack(states, axis=1).astype(x_prompt.dtype)
    return (y_prompt, y_sample, new_state)
```

```python
import functools
import math

import jax
import jax.numpy as jnp
from jax import lax
from jax.experimental import pallas as pl
from jax.experimental.pallas import tpu as pltpu

F32 = jnp.float32
BF16 = jnp.bfloat16

D_MODEL = 4096
N_PROMPT, T_PROMPT = 16, 256
N_SAMPLE, T_SAMPLE = 4, 1024
GRID_W = 64
DEPTH = 4
W_A = D_MODEL // 2
HEAD = 64
N_HEADS = W_A // HEAD
LORA_W = 128
LORA_A = 128
LORA_G = 256
RWKV_COLS = 3 * W_A + 2 * LORA_W + 2 * LORA_A + LORA_G
W_B = D_MODEL // 2
CONV_B = 31
HY_SHORT = 3
HY_EMB_DIM = 33
HY_TARGET = 1e-2
HY_MAX_DECAY = math.log(HY_TARGET) / 0.3
HY_MIN_DECAY = math.log(HY_TARGET) / 1.5
N_EXPERTS = 16
D_EXPERT = 1536
EC_FACTOR = 2
RMS_EPS = 1e-6
LN_EPS = 1e-5
GN_EPS = 64e-5

ROWS_P = N_PROMPT * T_PROMPT
ROWS_S = N_SAMPLE * T_SAMPLE
N_TOK = ROWS_P + ROWS_S
ROW_BLOCK = 256
N_COND = 8
LANES = 128
SUB = 8
KH = W_A // LANES
KL = LANES // N_HEADS
VMEM_LIMIT = 56 * 1024 * 1024

RUNS = ((0, N_PROMPT, T_PROMPT), (ROWS_P, N_SAMPLE, T_SAMPLE))


def _cparams(*sem):
    return pltpu.CompilerParams(dimension_semantics=sem, vmem_limit_bytes=VMEM_LIMIT)


def _sigmoid(x):
    return 1.0 / (1.0 + jnp.exp(-x))


def _mm_kernel(a_ref, w_ref, o_ref):
    o_ref[...] = jnp.dot(a_ref[...].astype(BF16), w_ref[...].astype(BF16),
                         preferred_element_type=F32).astype(o_ref.dtype)


def _wspec(w, layer, k, tn, off=0):
    if layer is None:
        return pl.BlockSpec((k, tn), lambda i, j: (0, off + j))
    return pl.BlockSpec((None, k, tn), lambda i, j: (layer, 0, off + j))


def _mm(a, w, *, tm, tn, layer=None, out_dtype=F32):
    m, k = a.shape
    n = w.shape[-1]
    tm = min(tm, m)
    assert w.shape[-2] == k and m % tm == 0 and n % tn == 0
    return pl.pallas_call(
        _mm_kernel,
        grid=(m // tm, n // tn),
        in_specs=[pl.BlockSpec((tm, k), lambda i, j: (i, 0)), _wspec(w, layer, k, tn)],
        out_specs=pl.BlockSpec((tm, tn), lambda i, j: (i, j)),
        out_shape=jax.ShapeDtypeStruct((m, n), out_dtype),
        compiler_params=_cparams("parallel", "arbitrary"),
        name="matmul",
    )(a, w)


def _glu_mm_kernel(a_ref, wa_ref, wb_ref, o_ref):
    a = a_ref[...]
    pa = jnp.dot(a, wa_ref[...].astype(BF16), preferred_element_type=F32)
    pb = jnp.dot(a, wb_ref[...].astype(BF16), preferred_element_type=F32)
    o_ref[...] = pa * _sigmoid(pb)


def _glu_mm(a, w, *, layer, col_a, col_b, n, tm, tn):
    m, k = a.shape
    tm = min(tm, m)
    oa, ob = col_a // tn, col_b // tn
    assert col_a % tn == 0 and col_b % tn == 0 and m % tm == 0 and n % tn == 0
    return pl.pallas_call(
        _glu_mm_kernel,
        grid=(m // tm, n // tn),
        in_specs=[pl.BlockSpec((tm, k), lambda i, j: (i, 0)), _wspec(w, layer, k, tn, oa), _wspec(w, layer, k, tn, ob)],
        out_specs=pl.BlockSpec((tm, tn), lambda i, j: (i, j)),
        out_shape=jax.ShapeDtypeStruct((m, n), F32),
        compiler_params=_cparams("parallel", "arbitrary"),
        name="glu_matmul",
    )(a, w, w)


def _mod_kernel(c_ref, w_ref, b_ref, o_ref):
    c = c_ref[...]
    o_ref[...] = jnp.dot(c * _sigmoid(c), w_ref[...], precision=lax.Precision.HIGHEST,
                         preferred_element_type=F32) + b_ref[...]


def _modulation(cond, w, b, layer, *, tn=512):
    n = w.shape[-1]
    return pl.pallas_call(
        _mod_kernel,
        grid=(n // tn,),
        in_specs=[pl.BlockSpec((N_COND, D_MODEL), lambda j: (0, 0)),
                  pl.BlockSpec((None, D_MODEL, tn), lambda j: (layer, 0, j)),
                  pl.BlockSpec((1, tn), lambda j: (0, j))],
        out_specs=pl.BlockSpec((N_COND, tn), lambda j: (0, j)),
        out_shape=jax.ShapeDtypeStruct((N_COND, n), F32),
        compiler_params=_cparams("arbitrary"),
        name="modulation",
    )(cond, w, b[None])


def _cond_of_block(i):
    pb = ROWS_P // ROW_BLOCK
    return jnp.where(i < pb, 0, 1 + (i - pb) // (T_SAMPLE // ROW_BLOCK))


def _expand_rows(m):
    return jnp.concatenate([jnp.broadcast_to(m[0], (ROWS_P, D_MODEL)),
                            jnp.repeat(m[1:1 + N_SAMPLE], T_SAMPLE, axis=0)], axis=0)


def _normed(x_ref, g_ref, sc_ref, sh_ref):
    x = x_ref[...]
    y = x * lax.rsqrt(jnp.mean(x * x, axis=-1, keepdims=True) + RMS_EPS) * g_ref[...]
    return y * (1.0 + sc_ref[...]) + sh_ref[...]


def _norm_kernel(x_ref, g_ref, sc_ref, sh_ref, o_ref):
    o_ref[...] = _normed(x_ref, g_ref, sc_ref, sh_ref).astype(o_ref.dtype)


def _norm_router_kernel(x_ref, g_ref, sc_ref, sh_ref, r_ref, o_ref, l_ref):
    h = _normed(x_ref, g_ref, sc_ref, sh_ref)
    o_ref[...] = h.astype(o_ref.dtype)
    l_ref[...] = jnp.dot(h, r_ref[...], precision=lax.Precision.HIGHEST, preferred_element_type=F32)


def _norm_mod(x, g, sc, sh, *, out_dtype, router=None):
    nblk = N_TOK // ROW_BLOCK
    row = pl.BlockSpec((ROW_BLOCK, D_MODEL), lambda i: (i, 0))
    vec = pl.BlockSpec((1, D_MODEL), lambda i: (0, 0))
    pat = pl.BlockSpec((None, 1, D_MODEL), lambda i: (_cond_of_block(i), 0, 0))
    in_specs = [row, vec, pat, pat]
    args = [x, g[None], sc[:, None, :], sh[:, None, :]]
    out_specs = row
    out_shape = jax.ShapeDtypeStruct((N_TOK, D_MODEL), out_dtype)
    body = _norm_kernel
    if router is not None:
        in_specs.append(pl.BlockSpec((D_MODEL, N_EXPERTS), lambda i: (0, 0)))
        args.append(router)
        out_specs = [row, pl.BlockSpec((ROW_BLOCK, N_EXPERTS), lambda i: (i, 0))]
        out_shape = [out_shape, jax.ShapeDtypeStruct((N_TOK, N_EXPERTS), F32)]
        body = _norm_router_kernel
    return pl.pallas_call(
        body, grid=(nblk,), in_specs=in_specs, out_specs=out_specs, out_shape=out_shape,
        compiler_params=_cparams("arbitrary"), name="norm_mod",
    )(*args)


def _group_sum(x):
    x = x + pltpu.roll(x, N_HEADS, 1)
    return x + pltpu.roll(x, 2 * N_HEADS, 1)


def _head_sum(x):
    acc = x[:, :LANES]
    for j in range(1, KH):
        acc = acc + x[:, j * LANES:(j + 1) * LANES]
    return _group_sum(acc)


def _prep_kernel(m_ref, wup_ref, w0_ref, aup_ref, a0_ref, gup_ref, kscale_ref, ka_ref, rk_ref,
                 kk_out, vrep_out, w_out, wr_out, b_out, kd_out, br_out, kr_out, g_out, bonus_out):
    r = m_ref[:, :W_A]
    k = m_ref[:, W_A:2 * W_A]
    v = m_ref[:, 2 * W_A:3 * W_A]
    o = 3 * W_A
    wd = jnp.tanh(m_ref[:, o:o + 2 * LORA_W]).astype(BF16)
    o += 2 * LORA_W
    ad = m_ref[:, o:o + 2 * LORA_A].astype(BF16)
    o += 2 * LORA_A
    gd = _sigmoid(m_ref[:, o:o + LORA_G]).astype(BF16)

    kx = k * kscale_ref[...]
    inv = jnp.minimum(lax.rsqrt(_head_sum(kx * kx)), 1e12)
    kk = kx * jnp.tile(inv, (1, KH))
    kk_out[...] = kk
    grp = lax.broadcasted_iota(jnp.int32, (v.shape[0], LANES), 1) // N_HEADS
    for j in range(KH):
        vj = v[:, j * LANES:(j + 1) * LANES]
        for g in range(KL):
            i = j * KL + g
            rep = _group_sum(jnp.where(grp == g, vj, 0.0))
            for rg in range(rep.shape[0] // SUB):
                vrep_out[rg, i] = rep[rg * SUB:(rg + 1) * SUB]
    g_out[...] = jnp.dot(gd, gup_ref[...].astype(BF16), preferred_element_type=F32)
    bonus = None
    for z in range(2):
        wl = w0_ref[z] + jnp.dot(wd[:, z * LORA_W:(z + 1) * LORA_W], wup_ref[z].astype(BF16),
                                 preferred_element_type=F32)
        neg = -wl
        softplus = jnp.maximum(neg, 0.0) + jnp.log(1.0 + jnp.exp(-jnp.abs(neg)))
        w = jnp.exp(-jnp.exp(-softplus - 0.5))
        a = _sigmoid(a0_ref[z] + jnp.dot(ad[:, z * LORA_A:(z + 1) * LORA_A], aup_ref[z].astype(BF16),
                                         preferred_element_type=F32))
        kd = k * (1.0 + (a - 1.0) * ka_ref[...])
        b = kk * a
        w_out[z] = w
        wr_out[z] = w * r
        b_out[z] = b
        kd_out[z] = kd
        br_out[z] = _head_sum(b * r)
        kr_out[z] = _head_sum(kd * r)
        hs = jnp.tile(_head_sum(r * kd * rk_ref[...]), (1, KH))
        bonus = hs if bonus is None else bonus + hs
    bonus_out[...] = bonus * v


def _rwkv_prep(mixed, prm, *, rb=128):
    rows = mixed.shape[0]
    full = lambda shape: pl.BlockSpec(shape, lambda i: (0,) * len(shape))
    one = pl.BlockSpec((rb, W_A), lambda i: (i, 0))
    two = pl.BlockSpec((2, rb, W_A), lambda i: (0, i, 0))
    dot2 = pl.BlockSpec((2, rb, LANES), lambda i: (0, i, 0))
    s1 = jax.ShapeDtypeStruct((rows, W_A), F32)
    s2 = jax.ShapeDtypeStruct((2, rows, W_A), F32)
    sd = jax.ShapeDtypeStruct((2, rows, LANES), F32)
    return pl.pallas_call(
        _prep_kernel,
        grid=(rows // rb,),
        in_specs=[pl.BlockSpec((rb, RWKV_COLS), lambda i: (i, 0)),
                  full((2, LORA_W, W_A)), full((2, 1, W_A)), full((2, LORA_A, W_A)), full((2, 1, W_A)),
                  full((LORA_G, W_A)), full((1, W_A)), full((1, W_A)), full((1, W_A))],
        out_specs=[one, pl.BlockSpec((rb // SUB, HEAD, SUB, LANES), lambda i: (i, 0, 0, 0)), two, two, two, two,
                   dot2, dot2, one, one],
        out_shape=[s1, jax.ShapeDtypeStruct((rows // SUB, HEAD, SUB, LANES), F32), s2, s2, s2, s2, sd, sd, s1, s1],
        compiler_params=_cparams("parallel"),
        name="rwkv_prep",
    )(mixed, prm["w_up"], prm["w0"][:, None, :], prm["a_up"], prm["a0"][:, None, :], prm["g_up"],
      prm["k_k"][None], prm["k_a"][None], prm["r_k"][None])


def _scan_step(op, vrep, br, kr, s_scr, sa_scr, y_scr, raw_scr, unroll):
    grp = lax.broadcasted_iota(jnp.int32, (SUB, LANES), 1) // N_HEADS

    def accumulate(ig):
        p1 = [None] * KL
        p2 = [None] * KL
        for j in range(KH):
            kkj = op(0, j)
            wrj = op(1, j)
            for g in range(KL):
                s = s_scr[ig * KL + g, j]
                a1 = s * kkj
                a2 = s * wrj
                p1[g] = a1 if p1[g] is None else p1[g] + a1
                p2[g] = a2 if p2[g] is None else p2[g] + a2
        for g in range(KL):
            raw_scr[0, ig * KL + g] = p1[g]
            raw_scr[1, ig * KL + g] = p2[g]

    def fold_once(ig):
        for g in range(KL):
            for q in range(2):
                x = raw_scr[q, ig * KL + g]
                raw_scr[q, ig * KL + g] = x + pltpu.roll(x, N_HEADS, 1)

    def fold_twice(ig):
        yv = None
        for g in range(KL):
            i = ig * KL + g
            x1 = raw_scr[0, i]
            x2 = raw_scr[1, i]
            sa = x1 + pltpu.roll(x1, 2 * N_HEADS, 1)
            sa_scr[i] = sa
            y_g = x2 + pltpu.roll(x2, 2 * N_HEADS, 1) - sa * br + vrep(i) * kr
            yv = y_g if yv is None else jnp.where(grp == g, y_g, yv)
        y_scr[ig] = yv

    lag = 2
    for n in range(KH + 2 * lag):
        if 2 * lag <= n:
            fold_twice(n - 2 * lag)
        if lag <= n < KH + lag:
            fold_once(n - lag)
        if n < KH:
            accumulate(n)

    def update_group(ig, c):
        sa = [sa_scr[ig * KL + g] for g in range(KL)]
        vi = [vrep(ig * KL + g) for g in range(KL)]
        for j in range(KH):
            wj = op(2, j)
            bj = op(3, j)
            kdj = op(4, j)
            for g in range(KL):
                idx = ig * KL + g
                s_scr[idx, j] = s_scr[idx, j] * wj + (vi[g] * kdj - sa[g] * bj)
        return c

    lax.fori_loop(0, KH, update_group, 0, unroll=unroll)


_SCAN_SCRATCH = [pltpu.VMEM((HEAD, KH, SUB, LANES), F32), pltpu.VMEM((HEAD, SUB, LANES), F32),
                 pltpu.VMEM((KH, SUB, LANES), F32), pltpu.VMEM((2, HEAD, SUB, LANES), F32)]


def _scan_prompt_kernel(kk_ref, vrep_ref, w_ref, wr_ref, b_ref, kd_ref, br_ref, kr_ref, y_ref, sfin_ref,
                        s_scr, sa_scr, y_scr, raw_scr, *, tb, nt, unroll):
    d = pl.program_id(0)
    tblk = pl.program_id(2)

    @pl.when(tblk == 0)
    def _():
        s_scr[...] = jnp.zeros(s_scr.shape, F32)

    def step(tt, c):
        t = tt + d * (tb - 1 - 2 * tt)
        refs = (kk_ref.at[t, 0], wr_ref.at[0, t, 0], w_ref.at[0, t, 0], b_ref.at[0, t, 0], kd_ref.at[0, t, 0])
        op = lambda slot, j: refs[slot][:, j * LANES:(j + 1) * LANES]
        vrep = lambda i: vrep_ref[t, 0, i]
        _scan_step(op, vrep, br_ref[0, t, 0], kr_ref[0, t, 0], s_scr, sa_scr, y_scr, raw_scr, unroll)
        for j in range(KH):
            y_ref[0, t, 0, :, j * LANES:(j + 1) * LANES] = y_scr[j]
        return c

    lax.fori_loop(0, tb, step, 0)

    @pl.when(tblk == nt - 1)
    def _():
        sfin_ref[0, 0] = s_scr[...]


def _rwkv_scan_prompt(kk, vrep, w, wr, b, kd, br, kr, *, tb=16, unroll=2):
    t, nb = kk.shape[0], kk.shape[1]
    nt = t // tb

    def tmap(d, j):
        return j + d * (nt - 1 - 2 * j)

    shared = pl.BlockSpec((tb, 1, SUB, W_A), lambda d, bi, j: (tmap(d, j), bi, 0, 0))
    values = pl.BlockSpec((tb, 1, HEAD, SUB, LANES), lambda d, bi, j: (tmap(d, j), bi, 0, 0, 0))
    perdir = pl.BlockSpec((1, tb, 1, SUB, W_A), lambda d, bi, j: (d, tmap(d, j), bi, 0, 0))
    dots = pl.BlockSpec((1, tb, 1, SUB, LANES), lambda d, bi, j: (d, tmap(d, j), bi, 0, 0))
    state = pl.BlockSpec((1, 1, HEAD, KH, SUB, LANES), lambda d, bi, j: (d, bi, 0, 0, 0, 0))
    return pl.pallas_call(
        functools.partial(_scan_prompt_kernel, tb=tb, nt=nt, unroll=unroll),
        grid=(2, nb, nt),
        in_specs=[shared, values, perdir, perdir, perdir, perdir, dots, dots],
        out_specs=[perdir, state],
        out_shape=[jax.ShapeDtypeStruct((2, t, nb, SUB, W_A), F32),
                   jax.ShapeDtypeStruct((2, nb, HEAD, KH, SUB, LANES), F32)],
        scratch_shapes=_SCAN_SCRATCH,
        compiler_params=_cparams("parallel", "parallel", "arbitrary"),
        name="rwkv_scan_prompt",
    )(kk, vrep, w, wr, b, kd, br, kr)


def _scan_sample_kernel(kkf, kkb, vf, vb, wf, wb, wrf, wrb, bf, bb, kdf, kdb, brf, brb, krf, krb, s0_ref,
                        yf_ref, yb_ref, s_scr, sa_scr, y_scr, raw_scr, ops_scr, v_scr, *, tb, unroll):
    tblk = pl.program_id(0)
    half = SUB // 2

    @pl.when(tblk == 0)
    def _():
        s_scr[...] = s0_ref[...]

    def merge(a8, b8, odd):
        low = lax.broadcasted_iota(jnp.int32, a8.shape, 0) < half
        if odd:
            return pltpu.roll(jnp.where(low, b8, a8), half, 0)
        return jnp.where(low, a8, b8)

    def two_steps(t2, c):
        for odd in (False, True):
            tt = 2 * t2 + (1 if odd else 0)
            gf = t2
            gb = (tb - 1 - tt) // 2
            pairs = ((kkf.at[gf], kkb.at[gb]), (wrf.at[0, gf], wrb.at[0, gb]), (wf.at[0, gf], wb.at[0, gb]),
                     (bf.at[0, gf], bb.at[0, gb]), (kdf.at[0, gf], kdb.at[0, gb]))
            for slot, (fr, br_) in enumerate(pairs):
                for j in range(KH):
                    sl = slice(j * LANES, (j + 1) * LANES)
                    ops_scr[slot, j] = merge(fr[:, sl], br_[:, sl], odd)

            def stage_values(i, c2):
                v_scr[i] = merge(vf[gf, i], vb[gb, i], odd)
                return c2

            lax.fori_loop(0, HEAD, stage_values, 0, unroll=8)
            br = merge(brf[0, gf], brb[0, gb], odd)
            kr = merge(krf[0, gf], krb[0, gb], odd)
            _scan_step(lambda slot, j: ops_scr[slot, j], lambda i: v_scr[i], br, kr, s_scr, sa_scr, y_scr, raw_scr,
                       unroll)
            for j in range(KH):
                sl = slice(j * LANES, (j + 1) * LANES)
                y8 = y_scr[j]
                if odd:
                    y8 = pltpu.roll(y8, half, 0)
                    yf_ref[gf, half:, sl] = y8[half:]
                    yb_ref[gb, :half, sl] = y8[:half]
                else:
                    yf_ref[gf, :half, sl] = y8[:half]
                    yb_ref[gb, half:, sl] = y8[half:]
        return c

    lax.fori_loop(0, tb // 2, two_steps, 0)


def _rwkv_scan_sample(kk, vrep, w, wr, b, kd, br, kr, s0, *, tb=16, unroll=2):
    t2 = kk.shape[0]
    nt = 2 * t2 // tb
    g = tb // 2
    fw = lambda j: j
    bw = lambda j: nt - 1 - j
    sh = lambda m: pl.BlockSpec((g, SUB, W_A), lambda j: (m(j), 0, 0))
    val = lambda m: pl.BlockSpec((g, HEAD, SUB, LANES), lambda j: (m(j), 0, 0, 0))
    pd = lambda d, m: pl.BlockSpec((1, g, SUB, W_A), lambda j: (d, m(j), 0, 0))
    dt = lambda d, m: pl.BlockSpec((1, g, SUB, LANES), lambda j: (d, m(j), 0, 0))
    st = pl.BlockSpec((HEAD, KH, SUB, LANES), lambda j: (0, 0, 0, 0))
    return pl.pallas_call(
        functools.partial(_scan_sample_kernel, tb=tb, unroll=unroll),
        grid=(nt,),
        in_specs=[sh(fw), sh(bw), val(fw), val(bw), pd(0, fw), pd(1, bw), pd(0, fw), pd(1, bw), pd(0, fw),
                  pd(1, bw), pd(0, fw), pd(1, bw), dt(0, fw), dt(1, bw), dt(0, fw), dt(1, bw), st],
        out_specs=[sh(fw), sh(bw)],
        out_shape=[jax.ShapeDtypeStruct((t2, SUB, W_A), F32), jax.ShapeDtypeStruct((t2, SUB, W_A), F32)],
        scratch_shapes=_SCAN_SCRATCH + [pltpu.VMEM((5, KH, SUB, LANES), F32), pltpu.VMEM((HEAD, SUB, LANES), F32)],
        compiler_params=_cparams("arbitrary"),
        name="rwkv_scan_sample",
    )(kk, kk, vrep, vrep, w, w, wr, wr, b, b, kd, kd, br, br, kr, kr, s0)


def _post_kernel(y0_ref, y1_ref, bonus_ref, g_ref, lg_ref, lb_ref, o_ref):
    y = y0_ref[...] + y1_ref[...]
    mu = jnp.tile(_head_sum(y) * (1.0 / HEAD), (1, KH))
    c = y - mu
    var = jnp.tile(_head_sum(c * c) * (1.0 / HEAD), (1, KH))
    yn = c * lax.rsqrt(var + GN_EPS) * lg_ref[...] + lb_ref[...]
    o_ref[...] = ((yn + bonus_ref[...]) * g_ref[...]).astype(o_ref.dtype)


def _rwkv_post(y0, y1, bonus, g, lnx_g, lnx_b, *, rb=256):
    rows = y0.shape[0]
    one = pl.BlockSpec((rb, W_A), lambda i: (i, 0))
    vec = pl.BlockSpec((1, W_A), lambda i: (0, 0))
    return pl.pallas_call(
        _post_kernel, grid=(rows // rb,), in_specs=[one, one, one, one, vec, vec], out_specs=one,
        out_shape=jax.ShapeDtypeStruct((rows, W_A), BF16),
        compiler_params=_cparams("parallel"), name="rwkv_post",
    )(y0, y1, bonus, g, lnx_g[None], lnx_b[None])


CONF_CHUNK = 512
CONF_NCH = W_B // CONF_CHUNK


def _conf_kernel(prev_ref, cur_ref, next_ref, w_ref, b_ref, lg_ref, lb_ref, o_ref, ext_scr, acc_scr,
                 *, nb, blocks):
    i = pl.program_id(0)
    rb = cur_ref.shape[0]
    has_prev = (i > 0).astype(F32)
    has_next = (i < blocks - 1).astype(F32)
    for c in range(CONF_NCH):
        cs = slice(c * CONF_CHUNK, (c + 1) * CONF_CHUNK)
        ext_scr[c, :rb] = prev_ref[:, cs] * has_prev
        ext_scr[c, rb:2 * rb] = cur_ref[:, cs]
        ext_scr[c, 2 * rb:] = next_ref[:, cs] * has_next
    pad = (CONV_B - 1) // 2

    def chunk(c, sums):
        acc = None
        for j in range(CONV_B):
            lo = rb + (j - pad) * nb
            term = ext_scr[c, lo:lo + rb] * w_ref[c, j:j + 1]
            acc = term if acc is None else acc + term
        acc = acc + b_ref[c]
        acc_scr[c] = acc
        return sums + jnp.sum(acc, axis=-1, keepdims=True)

    mu = lax.fori_loop(0, CONF_NCH, chunk, jnp.zeros((rb, 1), F32)) * (1.0 / W_B)
    var = jnp.zeros((rb, 1), F32)
    for c in range(CONF_NCH):
        d = acc_scr[c] - mu
        var = var + jnp.sum(d * d, axis=-1, keepdims=True)
    scale = lax.rsqrt(var * (1.0 / W_B) + LN_EPS)
    for c in range(CONF_NCH):
        y = (acc_scr[c] - mu) * scale * lg_ref[c] + lb_ref[c]
        o_ref[:, c * CONF_CHUNK:(c + 1) * CONF_CHUNK] = (y * _sigmoid(y)).astype(o_ref.dtype)


def _conformer(glu, prm, *, rb=256):
    chunked = lambda a: jnp.swapaxes(a.reshape(a.shape[0], CONF_NCH, CONF_CHUNK), 0, 1)
    outs = []
    for lo, nb, t in RUNS:
        rows = nb * t
        blocks = rows // rb
        base = lo // rb
        assert (CONV_B - 1) // 2 * nb <= rb
        spec = lambda f: pl.BlockSpec((rb, W_B), f)
        vec = lambda n: pl.BlockSpec((CONF_NCH, n, CONF_CHUNK), lambda i: (0, 0, 0))
        outs.append(pl.pallas_call(
            functools.partial(_conf_kernel, nb=nb, blocks=blocks),
            grid=(blocks,),
            in_specs=[spec(lambda i: (base + jnp.maximum(i - 1, 0), 0)), spec(lambda i: (base + i, 0)),
                      spec(lambda i: (base + jnp.minimum(i + 1, blocks - 1), 0)),
                      vec(CONV_B), vec(1), vec(1), vec(1)],
            out_specs=pl.BlockSpec((rb, W_B), lambda i: (i, 0)),
            out_shape=jax.ShapeDtypeStruct((rows, W_B), BF16),
            scratch_shapes=[pltpu.VMEM((CONF_NCH, 3 * rb, CONF_CHUNK), F32),
                            pltpu.VMEM((CONF_NCH, rb, CONF_CHUNK), F32)],
            compiler_params=_cparams("parallel"),
            name="conformer_conv",
        )(glu, glu, glu, chunked(prm["dw_w"]), chunked(prm["dw_b"][None]), chunked(prm["ln_g"][None]),
          chunked(prm["ln_b"][None])))
    return jnp.concatenate(outs, axis=0)


def _hyena_kernel(x0_ref, x1_ref, v_ref, w0_ref, w1_ref, wv_ref, b0_ref, b1_ref, bv_ref, bias_ref, kf_ref,
                  fwd_ref, inv_ref, o_ref):
    length = x0_ref.shape[0]
    row = lax.broadcasted_iota(jnp.int32, x0_ref.shape, 0)

    def short(u_ref, w_ref, b_ref):
        u = u_ref[...]
        prev = jnp.where(row == 0, 0.0, pltpu.roll(u, 1, 0))
        nxt = jnp.where(row == length - 1, 0.0, pltpu.roll(u, length - 1, 0))
        return prev * w_ref[0:1] + u * w_ref[1:2] + nxt * w_ref[2:3] + b_ref[...]

    x1 = short(x1_ref, w1_ref, b1_ref)
    v = short(v_ref, wv_ref, bv_ref)
    z = v * x1
    zf = jnp.dot(fwd_ref[...], z.astype(BF16), preferred_element_type=F32)
    zc, zs = zf[:length], zf[length:]
    kc, ks = kf_ref[:length], kf_ref[length:]
    first = row == 0
    yc = jnp.where(first, zc * kc, zc * kc - zs * ks)
    ys = jnp.where(first, zs * ks, zc * ks + zs * kc)
    yf = jnp.concatenate([yc, ys], axis=0).astype(BF16)
    y = jnp.dot(inv_ref[...], yf, preferred_element_type=F32)
    x0 = short(x0_ref, w0_ref, b0_ref)
    o_ref[...] = ((y + bias_ref[...] * z) * x0).astype(o_ref.dtype)


def _dft_tables(length):
    n = 2 * length
    kidx = jnp.arange(length, dtype=jnp.int32)[:, None]
    sidx = jnp.arange(n, dtype=jnp.int32)[None, :]
    ang = (2.0 * math.pi / n) * ((kidx * sidx) % n).astype(F32)
    cos = jnp.cos(ang)
    sin = jnp.where(kidx == 0, jnp.where(sidx % 2 == 0, 1.0, -1.0).astype(F32), jnp.sin(ang))
    fwd = jnp.concatenate([cos, sin], axis=0)
    scale = jnp.where(kidx == 0, 1.0 / n, 2.0 / n)
    inv = jnp.concatenate([(cos * scale)[:, :length].T, (sin * scale)[:, :length].T], axis=1)
    return fwd, inv


def _hyena_filters(length, prm):
    hi = lax.Precision.HIGHEST
    t = jnp.linspace(0.0, 1.0, length, dtype=F32)[:, None]
    bands = (HY_EMB_DIM - 1) // 2
    ang = (2.0 * math.pi / length) * jnp.arange(length, dtype=F32)[:, None] \
        * jnp.linspace(1e-4, bands - 1, bands, dtype=F32)[None, :]
    z = jnp.concatenate([t, jnp.cos(ang), -jnp.sin(ang)], axis=-1)
    fr = prm["f_freq"]
    h = jnp.sin(fr * (jnp.dot(z, prm["f_w1"], precision=hi) + prm["f_b1"]))
    h = jnp.sin(fr * (jnp.dot(h, prm["f_w2"], precision=hi) + prm["f_b2"]))
    h = jnp.sin(fr * (jnp.dot(h, prm["f_w3"], precision=hi) + prm["f_b3"]))
    h = jnp.dot(h, prm["f_w4"], precision=hi).reshape(length, 2, D_MODEL)
    deltas = jnp.abs(jnp.linspace(HY_MIN_DECAY, HY_MAX_DECAY, D_MODEL, dtype=F32))
    h = h * jnp.exp(-t * deltas)[:, None, :]
    h = h / (jnp.sum(jnp.abs(h), axis=(0, 1)) + 1e-6)
    return h[:, 0], h[:, 1]


def _hyena(u, prm):
    outs = []
    for lo, nb, t in RUNS:
        tn = 1024 if t <= 256 else 256
        per = D_MODEL // tn
        hf, hb = _hyena_filters(t, prm)
        fwd, inv = _dft_tables(t)
        kern = jnp.concatenate([hf, jnp.zeros((1, D_MODEL), F32), jnp.flip(hb[1:], axis=0)], axis=0)
        kf = _mm(fwd, kern, tm=min(2 * t, 1024), tn=1024)
        first = lo // t
        part = lambda p: pl.BlockSpec((t, tn), lambda b, j: (first + b, p * per + j))
        wpart = lambda p, n: pl.BlockSpec((n, tn), lambda b, j: (0, p * per + j))
        chan = lambda n: pl.BlockSpec((n, tn), lambda b, j: (0, j))
        const = lambda s: pl.BlockSpec(s, lambda b, j: (0, 0))
        out = pl.pallas_call(
            _hyena_kernel,
            grid=(nb, per),
            in_specs=[part(0), part(1), part(2), wpart(0, HY_SHORT), wpart(1, HY_SHORT), wpart(2, HY_SHORT),
                      wpart(0, 1), wpart(1, 1), wpart(2, 1), chan(1), chan(2 * t),
                      const((2 * t, t)), const((t, 2 * t))],
            out_specs=pl.BlockSpec((t, tn), lambda b, j: (b, j)),
            out_shape=jax.ShapeDtypeStruct((nb * t, D_MODEL), BF16),
            compiler_params=_cparams("parallel", "arbitrary"),
            name="hyena_conv",
        )(u, u, u, prm["sc_w"], prm["sc_w"], prm["sc_w"], prm["sc_b"][None], prm["sc_b"][None],
          prm["sc_b"][None], prm["bias"][None], kf, fwd[:, :t].astype(BF16), inv.astype(BF16))
        outs.append(out)
    return jnp.concatenate(outs, axis=0)


def _moe_up_kernel(x_ref, wg_ref, wu_ref, h_ref):
    x = x_ref[0]
    g = jnp.dot(x, wg_ref[0].astype(BF16), preferred_element_type=F32)
    u = jnp.dot(x, wu_ref[0].astype(BF16), preferred_element_type=F32)
    h_ref[0] = (g * _sigmoid(g) * u).astype(h_ref.dtype)


def _moe_down_kernel(h_ref, wd_ref, gate_ref, op_ref, os_ref):
    res = jnp.dot(h_ref[0], wd_ref[0].astype(BF16), preferred_element_type=F32) * gate_ref[0]
    n_p = op_ref.shape[0] * op_ref.shape[1]
    op_ref[...] = res[:n_p].reshape(op_ref.shape)
    os_ref[...] = res[n_p:].reshape(os_ref.shape)


def _moe_experts(xg, gate, wg, wu, wd, layer, *, tf=256, tn=512):
    e, rows, _ = xg.shape
    caps = [EC_FACTOR * t // N_EXPERTS for _, _, t in RUNS]
    h = pl.pallas_call(
        _moe_up_kernel,
        grid=(e, D_EXPERT // tf),
        in_specs=[pl.BlockSpec((1, rows, D_MODEL), lambda ei, f: (ei, 0, 0)),
                  pl.BlockSpec((None, 1, D_MODEL, tf), lambda ei, f: (layer, ei, 0, f)),
                  pl.BlockSpec((None, 1, D_MODEL, tf), lambda ei, f: (layer, ei, 0, f))],
        out_specs=pl.BlockSpec((1, rows, tf), lambda ei, f: (ei, 0, f)),
        out_shape=jax.ShapeDtypeStruct((e, rows, D_EXPERT), BF16),
        compiler_params=_cparams("parallel", "arbitrary"),
        name="moe_up",
    )(xg, wg, wu)
    seq_major = lambda nb, cap: pl.BlockSpec((nb, None, cap, tn), lambda ei, n: (0, ei, 0, n))
    return pl.pallas_call(
        _moe_down_kernel,
        grid=(e, D_MODEL // tn),
        in_specs=[pl.BlockSpec((1, rows, D_EXPERT), lambda ei, n: (ei, 0, 0)),
                  pl.BlockSpec((None, 1, D_EXPERT, tn), lambda ei, n: (layer, ei, 0, n)),
                  pl.BlockSpec((1, rows, 1), lambda ei, n: (ei, 0, 0))],
        out_specs=[seq_major(N_PROMPT, caps[0]), seq_major(N_SAMPLE, caps[1])],
        out_shape=[jax.ShapeDtypeStruct((N_PROMPT, e, caps[0], D_MODEL), F32),
                   jax.ShapeDtypeStruct((N_SAMPLE, e, caps[1], D_MODEL), F32)],
        compiler_params=_cparams("parallel", "arbitrary"),
        name="moe_down",
    )(h, wd, gate)


def _combine_kernel(idx_ref, o_ref, g_ref, x_ref, y_ref, onehot_scr):
    @pl.when(pl.program_id(1) == 0)
    def _():
        tok = lax.broadcasted_iota(jnp.int32, onehot_scr.shape, 0)
        onehot_scr[...] = jnp.where(tok == idx_ref[...], 1.0, 0.0).astype(BF16)

    onehot = onehot_scr[...]
    o = o_ref[...]
    hi = o.astype(BF16)
    lo = (o - hi.astype(F32)).astype(BF16)
    mix = jnp.dot(onehot, hi, preferred_element_type=F32) + jnp.dot(onehot, lo, preferred_element_type=F32)
    y_ref[...] = x_ref[...] + g_ref[...] * mix


def _moe_combine(x, gate_rows, outs, idxs, *, tn=512):
    for run, (lo, nb, t) in enumerate(RUNS):
        o = outs[run].reshape(nb, -1, D_MODEL)
        slots = o.shape[1]
        first = lo // t
        cond = (lambda b: 0) if run == 0 else (lambda b: 1 + b)
        rows = pl.BlockSpec((t, tn), lambda b, j: (first + b, j))
        x = pl.pallas_call(
            _combine_kernel,
            grid=(nb, D_MODEL // tn),
            in_specs=[pl.BlockSpec((None, 1, slots), lambda b, j: (b, 0, 0)),
                      pl.BlockSpec((None, slots, tn), lambda b, j: (b, 0, j)),
                      pl.BlockSpec((None, 1, tn), lambda b, j: (cond(b), 0, j)),
                      rows],
            out_specs=rows,
            out_shape=jax.ShapeDtypeStruct((N_TOK, D_MODEL), F32),
            scratch_shapes=[pltpu.VMEM((t, slots), BF16)],
            input_output_aliases={3: 0},
            compiler_params=_cparams("parallel", "arbitrary"),
            name="moe_combine",
        )(idxs[run].reshape(nb, 1, slots), o, gate_rows[:, None, :], x)
    return x


def _ec_moe(x, gate_rows, h, logits, wg, wu, wd, layer):
    aff = jax.nn.softmax(logits, axis=-1)
    gates, rows, idxs = [], [], []
    for lo, nb, t in RUNS:
        cap = EC_FACTOR * t // N_EXPERTS
        a = jnp.swapaxes(aff[lo:lo + nb * t].reshape(nb, t, N_EXPERTS), 1, 2)
        gate, idx = lax.top_k(a, cap)
        idxs.append(idx)
        glob = lo + idx + (jnp.arange(nb, dtype=jnp.int32) * t)[:, None, None]
        rows.append(jnp.swapaxes(glob, 0, 1).reshape(N_EXPERTS, nb * cap))
        gates.append(jnp.swapaxes(gate, 0, 1).reshape(N_EXPERTS, nb * cap))
    rows = jnp.concatenate(rows, axis=1)
    gate = jnp.concatenate(gates, axis=1)
    outs = _moe_experts(h[rows], gate[..., None], wg, wu, wd, layer)
    return _moe_combine(x, gate_rows, outs, idxs)


def _key_major(x):
    return jnp.swapaxes(x.reshape(x.shape[:-1] + (N_HEADS, HEAD)), -1, -2).reshape(x.shape)


def _to_time_major(x):
    return jnp.concatenate([jnp.swapaxes(x[lo:lo + nb * t].reshape(nb, t, -1), 0, 1).reshape(nb * t, -1)
                            for lo, nb, t in RUNS], axis=0)


def _to_seq_major(x):
    return jnp.concatenate([jnp.swapaxes(x[lo:lo + nb * t].reshape(t, nb, -1), 0, 1).reshape(nb * t, -1)
                            for lo, nb, t in RUNS], axis=0)


def _shift_steps(p, steps):
    if steps > 0:
        return jnp.pad(p[:-steps], ((steps, 0), (0, 0), (0, 0)))
    return jnp.pad(p[-steps:], ((0, -steps), (0, 0), (0, 0)))


def _token_shift_mix(p, quad, mu):
    outs = []
    for lo, nb, t in RUNS:
        q = p[lo:lo + nb * t].reshape(t, nb, RWKV_COLS)
        if t == T_PROMPT:
            shifted = jnp.where(quad % 2 == 0, _shift_steps(q, 1), _shift_steps(q, -1))
        else:
            col = (lax.broadcasted_iota(jnp.int32, (t, 1, 1), 0) % GRID_W)
            left = jnp.where(col == 0, 0.0, _shift_steps(q, 1))
            right = jnp.where(col == GRID_W - 1, 0.0, _shift_steps(q, -1))
            shifted = jnp.where(quad == 0, left, jnp.where(quad == 1, right, jnp.where(
                quad == 2, _shift_steps(q, GRID_W), _shift_steps(q, -GRID_W))))
        outs.append((q + (shifted - q) * mu).reshape(nb * t, RWKV_COLS))
    return outs


def _rwkv_mixer(mixed, s0_sample, prm):
    kk, vrep, w, wr, b, kd, br, kr, g, bonus = _rwkv_prep(mixed[0], prm)
    nbt = N_PROMPT // SUB
    tile = lambda x: x.reshape(x.shape[:-2] + (T_PROMPT, nbt, SUB, x.shape[-1]))
    y_p, s_p = _rwkv_scan_prompt(tile(kk), vrep.reshape(T_PROMPT, nbt, HEAD, SUB, LANES), tile(w), tile(wr),
                                 tile(b), tile(kd), tile(br), tile(kr))
    y_p = y_p.reshape(2, ROWS_P, W_A)
    y_ap = _rwkv_post(y_p[0], y_p[1], bonus, g, prm["lnx_g"], prm["lnx_b"])
    s_p = s_p.reshape(2, nbt, HEAD, KH, SUB, KL, N_HEADS)
    s_p = jnp.transpose(s_p, (1, 4, 0, 6, 2, 3, 5)).reshape(N_PROMPT, 2, N_HEADS, HEAD, HEAD)
    kk, vrep, w, wr, b, kd, br, kr, g, bonus = _rwkv_prep(mixed[1], prm)
    tile = lambda x: x.reshape(x.shape[:-2] + (T_SAMPLE // 2, SUB, x.shape[-1]))
    s0 = s0_sample.reshape(N_SAMPLE, 2, N_HEADS, HEAD, KH, KL)
    s0 = jnp.transpose(s0, (3, 4, 1, 0, 5, 2)).reshape(HEAD, KH, SUB, LANES)
    yf, yb = _rwkv_scan_sample(tile(kk), vrep, tile(w), tile(wr), tile(b), tile(kd), tile(br), tile(kr), s0)
    y_as = _rwkv_post(yf.reshape(ROWS_S, W_A), yb.reshape(ROWS_S, W_A), bonus, g, prm["lnx_g"], prm["lnx_b"])
    return jnp.concatenate([y_ap, y_as], axis=0), s_p


def kernel(x_prompt, x_sample, state_rwkv, c, c_ctx, norm_g, mod_w, mod_b, ab_w_in, ab_mu, ab_w_up, ab_w0, ab_a_up, ab_a0, ab_g_up, ab_k_k, ab_k_a, ab_r_k, ab_lnx_g, ab_lnx_b, ab_dw_w, ab_dw_b, ab_ln_g, ab_ln_b, ab_w_out, hy_w_in, hy_sc_w, hy_sc_b, hy_f_w1, hy_f_b1, hy_f_w2, hy_f_b2, hy_f_w3, hy_f_b3, hy_f_w4, hy_f_freq, hy_bias, hy_w_out, moe_router, moe_w_gate, moe_w_up, moe_w_down, final_g):
    x = jnp.concatenate([x_prompt.reshape(ROWS_P, D_MODEL), x_sample.reshape(ROWS_S, D_MODEL)], axis=0)
    cond = jnp.concatenate([c_ctx[None], c, jnp.zeros((N_COND - 1 - N_SAMPLE, D_MODEL), F32)], axis=0)
    chan = jnp.arange(RWKV_COLS, dtype=jnp.int32)
    quad = jnp.concatenate([_key_major(chan[:3 * W_A].reshape(3, W_A)).reshape(-1), chan[3 * W_A:]]) % 4
    states = []
    for l in range(DEPTH):
        mod = _modulation(cond, mod_w, mod_b[l], l)
        sh1, sc1, g1, sh2, sc2, g2 = jnp.split(mod, 6, axis=-1)
        h = _norm_mod(x, norm_g[l, 0], sc1, sh1, out_dtype=BF16)
        i = l // 2
        if l % 2 == 0:
            km3 = lambda a: jnp.concatenate([_key_major(a[..., :3 * W_A].reshape(a.shape[:-1] + (3, W_A))).reshape(
                a.shape[:-1] + (3 * W_A,)), a[..., 3 * W_A:]], axis=-1)
            w_rwkv = km3(ab_w_in[i][:, :RWKV_COLS])
            prm = dict(w_up=_key_major(ab_w_up[i]), w0=_key_major(ab_w0[i]), a_up=_key_major(ab_a_up[i]),
                       a0=_key_major(ab_a0[i]), g_up=_key_major(ab_g_up[i]), k_k=_key_major(ab_k_k[i]),
                       k_a=_key_major(ab_k_a[i]), r_k=_key_major(ab_r_k[i].reshape(W_A)),
                       lnx_g=_key_major(ab_lnx_g[i]), lnx_b=_key_major(ab_lnx_b[i]),
                       dw_w=ab_dw_w[i], dw_b=ab_dw_b[i], ln_g=ab_ln_g[i], ln_b=ab_ln_b[i])
            h_tm = _to_time_major(h)
            proj = _mm(h_tm, w_rwkv, tm=2048, tn=256)
            mixed = _token_shift_mix(proj, quad, km3(ab_mu[i]))
            y_a, s_fin = _rwkv_mixer(mixed, state_rwkv[:, i], prm)
            states.append(s_fin)
            glu = _glu_mm(h_tm, ab_w_in, layer=i, col_a=RWKV_COLS, col_b=RWKV_COLS + W_B, n=W_B, tm=1024, tn=256)
            cv = _conformer(glu, prm)
            w_out = jnp.concatenate([jnp.swapaxes(ab_w_out[i][:W_A].reshape(N_HEADS, HEAD, D_MODEL), 0, 1).reshape(
                W_A, D_MODEL), ab_w_out[i][W_A:]], axis=0)
            y = _mm(_to_seq_major(jnp.concatenate([y_a, cv], axis=-1)), w_out, tm=2048, tn=256)
        else:
            u = _mm(h, hy_w_in, layer=i, tm=2048, tn=256)
            prm = dict(sc_w=hy_sc_w[i], sc_b=hy_sc_b[i], f_w1=hy_f_w1[i], f_b1=hy_f_b1[i], f_w2=hy_f_w2[i],
                       f_b2=hy_f_b2[i], f_w3=hy_f_w3[i], f_b3=hy_f_b3[i], f_w4=hy_f_w4[i], f_freq=hy_f_freq[i],
                       bias=hy_bias[i])
            y = _mm(_hyena(u, prm), hy_w_out, layer=i, tm=2048, tn=256)
        x = x + _expand_rows(g1) * y
        h2, logits = _norm_mod(x, norm_g[l, 1], sc2, sh2, out_dtype=BF16, router=moe_router[l])
        x = _ec_moe(x, g2, h2, logits, moe_w_gate, moe_w_up, moe_w_down, l)
    zeros = jnp.zeros((N_COND, D_MODEL), F32)
    y = _norm_mod(x, final_g, zeros, zeros, out_dtype=F32)
    y_prompt = y[:ROWS_P].reshape(N_PROMPT, T_PROMPT, D_MODEL)
    y_sample = y[ROWS_P:].reshape(N_SAMPLE, T_SAMPLE, D_MODEL)
    new_state = jnp.stack(states, axis=1).astype(x_prompt.dtype)
    return (y_prompt, y_sample, new_state)
```

```python
import functools
import math

import jax
import jax.numpy as jnp
from jax import lax
from jax.experimental import pallas as pl
from jax.experimental.pallas import tpu as pltpu

F32 = jnp.float32
BF16 = jnp.bfloat16

D_MODEL = 4096
N_PROMPT, T_PROMPT = 16, 256
N_SAMPLE, T_SAMPLE = 4, 1024
GRID_W = 64
DEPTH = 4
W_A = D_MODEL // 2
HEAD = 64
N_HEADS = W_A // HEAD
LORA_W = 128
LORA_A = 128
LORA_G = 256
RWKV_COLS = 3 * W_A + 2 * LORA_W + 2 * LORA_A + LORA_G
W_B = D_MODEL // 2
CONV_B = 31
HY_SHORT = 3
HY_EMB_DIM = 33
HY_TARGET = 1e-2
HY_MAX_DECAY = math.log(HY_TARGET) / 0.3
HY_MIN_DECAY = math.log(HY_TARGET) / 1.5
N_EXPERTS = 16
D_EXPERT = 1536
EC_FACTOR = 2
RMS_EPS = 1e-6
LN_EPS = 1e-5
GN_EPS = 64e-5

ROWS_P = N_PROMPT * T_PROMPT
ROWS_S = N_SAMPLE * T_SAMPLE
N_TOK = ROWS_P + ROWS_S
ROW_BLOCK = 256
N_COND = 8
LANES = 128
SUB = 8
KH = W_A // LANES
KL = LANES // N_HEADS
VMEM_LIMIT = 56 * 1024 * 1024

RUNS = ((0, N_PROMPT, T_PROMPT), (ROWS_P, N_SAMPLE, T_SAMPLE))


def _cparams(*sem):
    return pltpu.CompilerParams(dimension_semantics=sem, vmem_limit_bytes=VMEM_LIMIT)


def _sigmoid(x):
    return 1.0 / (1.0 + jnp.exp(-x))


def _mm_kernel(a_ref, w_ref, o_ref):
    o_ref[...] = jnp.dot(a_ref[...].astype(BF16), w_ref[...].astype(BF16),
                         preferred_element_type=F32).astype(o_ref.dtype)


def _wspec(w, layer, k, tn, off=0):
    if layer is None:
        return pl.BlockSpec((k, tn), lambda i, j: (0, off + j))
    return pl.BlockSpec((None, k, tn), lambda i, j: (layer, 0, off + j))


def _mm(a, w, *, tm, tn, layer=None, out_dtype=F32):
    m, k = a.shape
    n = w.shape[-1]
    tm = min(tm, m)
    assert w.shape[-2] == k and m % tm == 0 and n % tn == 0
    return pl.pallas_call(
        _mm_kernel,
        grid=(m // tm, n // tn),
        in_specs=[pl.BlockSpec((tm, k), lambda i, j: (i, 0)), _wspec(w, layer, k, tn)],
        out_specs=pl.BlockSpec((tm, tn), lambda i, j: (i, j)),
        out_shape=jax.ShapeDtypeStruct((m, n), out_dtype),
        compiler_params=_cparams("parallel", "arbitrary"),
        name="matmul",
    )(a, w)


def _mm_res_kernel(a_ref, w_ref, x_ref, g_ref, o_ref):
    y = jnp.dot(a_ref[...].astype(BF16), w_ref[...].astype(BF16), preferred_element_type=F32)
    o_ref[...] = x_ref[...] + g_ref[...] * y


def _mm_residual(a, w, x, gate_rows, *, tn, layer=None):
    m, k = a.shape
    n = w.shape[-1]
    tm = T_SAMPLE
    pb = ROWS_P // tm
    assert w.shape[-2] == k and ROWS_P % tm == 0 and m % tm == 0 and n % tn == 0
    tile = pl.BlockSpec((tm, tn), lambda i, j: (i, j))
    return pl.pallas_call(
        _mm_res_kernel,
        grid=(m // tm, n // tn),
        in_specs=[pl.BlockSpec((tm, k), lambda i, j: (i, 0)), _wspec(w, layer, k, tn), tile,
                  pl.BlockSpec((None, 1, tn), lambda i, j: (jnp.where(i < pb, 0, 1 + i - pb), 0, j))],
        out_specs=tile,
        out_shape=jax.ShapeDtypeStruct((m, n), F32),
        input_output_aliases={2: 0},
        compiler_params=_cparams("parallel", "arbitrary"),
        name="matmul_residual",
    )(a, w, x, gate_rows[:, None, :])


def _glu_mm_kernel(a_ref, wa_ref, wb_ref, o_ref):
    a = a_ref[...]
    pa = jnp.dot(a, wa_ref[...].astype(BF16), preferred_element_type=F32)
    pb = jnp.dot(a, wb_ref[...].astype(BF16), preferred_element_type=F32)
    o_ref[...] = pa * _sigmoid(pb)


def _glu_mm(a, w, *, layer, col_a, col_b, n, tm, tn):
    m, k = a.shape
    tm = min(tm, m)
    oa, ob = col_a // tn, col_b // tn
    assert col_a % tn == 0 and col_b % tn == 0 and m % tm == 0 and n % tn == 0
    return pl.pallas_call(
        _glu_mm_kernel,
        grid=(m // tm, n // tn),
        in_specs=[pl.BlockSpec((tm, k), lambda i, j: (i, 0)), _wspec(w, layer, k, tn, oa), _wspec(w, layer, k, tn, ob)],
        out_specs=pl.BlockSpec((tm, tn), lambda i, j: (i, j)),
        out_shape=jax.ShapeDtypeStruct((m, n), F32),
        compiler_params=_cparams("parallel", "arbitrary"),
        name="glu_matmul",
    )(a, w, w)


def _mod_kernel(c_ref, w_ref, b_ref, o_ref):
    c = c_ref[...]
    o_ref[...] = jnp.dot(c * _sigmoid(c), w_ref[...], precision=lax.Precision.HIGHEST,
                         preferred_element_type=F32) + b_ref[...]


def _modulation(cond, w, b, layer, *, tn=512):
    n = w.shape[-1]
    return pl.pallas_call(
        _mod_kernel,
        grid=(n // tn,),
        in_specs=[pl.BlockSpec((N_COND, D_MODEL), lambda j: (0, 0)),
                  pl.BlockSpec((None, D_MODEL, tn), lambda j: (layer, 0, j)),
                  pl.BlockSpec((1, tn), lambda j: (0, j))],
        out_specs=pl.BlockSpec((N_COND, tn), lambda j: (0, j)),
        out_shape=jax.ShapeDtypeStruct((N_COND, n), F32),
        compiler_params=_cparams("arbitrary"),
        name="modulation",
    )(cond, w, b[None])


def _cond_of_block(i):
    pb = ROWS_P // ROW_BLOCK
    return jnp.where(i < pb, 0, 1 + (i - pb) // (T_SAMPLE // ROW_BLOCK))


def _normed(x_ref, g_ref, sc_ref, sh_ref):
    x = x_ref[...]
    y = x * lax.rsqrt(jnp.mean(x * x, axis=-1, keepdims=True) + RMS_EPS) * g_ref[...]
    return y * (1.0 + sc_ref[...]) + sh_ref[...]


def _norm_kernel(x_ref, g_ref, sc_ref, sh_ref, o_ref):
    o_ref[...] = _normed(x_ref, g_ref, sc_ref, sh_ref).astype(o_ref.dtype)


def _norm_router_kernel(x_ref, g_ref, sc_ref, sh_ref, r_ref, o_ref, l_ref):
    h = _normed(x_ref, g_ref, sc_ref, sh_ref)
    o_ref[...] = h.astype(o_ref.dtype)
    l_ref[...] = jnp.dot(h, r_ref[...], precision=lax.Precision.HIGHEST, preferred_element_type=F32)


def _norm_mod(x, g, sc, sh, *, out_dtype, router=None):
    nblk = N_TOK // ROW_BLOCK
    row = pl.BlockSpec((ROW_BLOCK, D_MODEL), lambda i: (i, 0))
    vec = pl.BlockSpec((1, D_MODEL), lambda i: (0, 0))
    pat = pl.BlockSpec((None, 1, D_MODEL), lambda i: (_cond_of_block(i), 0, 0))
    in_specs = [row, vec, pat, pat]
    args = [x, g[None], sc[:, None, :], sh[:, None, :]]
    out_specs = row
    out_shape = jax.ShapeDtypeStruct((N_TOK, D_MODEL), out_dtype)
    body = _norm_kernel
    if router is not None:
        in_specs.append(pl.BlockSpec((D_MODEL, N_EXPERTS), lambda i: (0, 0)))
        args.append(router)
        out_specs = [row, pl.BlockSpec((ROW_BLOCK, N_EXPERTS), lambda i: (i, 0))]
        out_shape = [out_shape, jax.ShapeDtypeStruct((N_TOK, N_EXPERTS), F32)]
        body = _norm_router_kernel
    return pl.pallas_call(
        body, grid=(nblk,), in_specs=in_specs, out_specs=out_specs, out_shape=out_shape,
        compiler_params=_cparams("arbitrary"), name="norm_mod",
    )(*args)


def _group_sum(x):
    x = x + pltpu.roll(x, N_HEADS, 1)
    return x + pltpu.roll(x, 2 * N_HEADS, 1)


def _head_sum(x):
    acc = x[:, :LANES]
    for j in range(1, KH):
        acc = acc + x[:, j * LANES:(j + 1) * LANES]
    return _group_sum(acc)


def _prep_kernel(m_ref, wup_ref, w0_ref, aup_ref, a0_ref, gup_ref, kscale_ref, ka_ref, rk_ref,
                 kk_out, vrep_out, w_out, wr_out, b_out, kd_out, br_out, kr_out, g_out, bonus_out):
    r = m_ref[:, :W_A]
    k = m_ref[:, W_A:2 * W_A]
    v = m_ref[:, 2 * W_A:3 * W_A]
    o = 3 * W_A
    wd = jnp.tanh(m_ref[:, o:o + 2 * LORA_W]).astype(BF16)
    o += 2 * LORA_W
    ad = m_ref[:, o:o + 2 * LORA_A].astype(BF16)
    o += 2 * LORA_A
    gd = _sigmoid(m_ref[:, o:o + LORA_G]).astype(BF16)

    kx = k * kscale_ref[...]
    inv = jnp.minimum(lax.rsqrt(_head_sum(kx * kx)), 1e12)
    kk = kx * jnp.tile(inv, (1, KH))
    kk_out[...] = kk
    grp = lax.broadcasted_iota(jnp.int32, (v.shape[0], LANES), 1) // N_HEADS
    for j in range(KH):
        vj = v[:, j * LANES:(j + 1) * LANES]
        for g in range(KL):
            i = j * KL + g
            rep = _group_sum(jnp.where(grp == g, vj, 0.0))
            for rg in range(rep.shape[0] // SUB):
                vrep_out[rg, i] = rep[rg * SUB:(rg + 1) * SUB]
    g_out[...] = jnp.dot(gd, gup_ref[...].astype(BF16), preferred_element_type=F32)
    bonus = None
    for z in range(2):
        wl = w0_ref[z] + jnp.dot(wd[:, z * LORA_W:(z + 1) * LORA_W], wup_ref[z].astype(BF16),
                                 preferred_element_type=F32)
        neg = -wl
        softplus = jnp.maximum(neg, 0.0) + jnp.log(1.0 + jnp.exp(-jnp.abs(neg)))
        w = jnp.exp(-jnp.exp(-softplus - 0.5))
        a = _sigmoid(a0_ref[z] + jnp.dot(ad[:, z * LORA_A:(z + 1) * LORA_A], aup_ref[z].astype(BF16),
                                         preferred_element_type=F32))
        kd = k * (1.0 + (a - 1.0) * ka_ref[...])
        b = kk * a
        w_out[z] = w
        wr_out[z] = w * r
        b_out[z] = b
        kd_out[z] = kd
        br_out[z] = _head_sum(b * r)
        kr_out[z] = _head_sum(kd * r)
        hs = jnp.tile(_head_sum(r * kd * rk_ref[...]), (1, KH))
        bonus = hs if bonus is None else bonus + hs
    bonus_out[...] = bonus * v


def _rwkv_prep(mixed, prm, *, rb=128):
    rows = mixed.shape[0]
    full = lambda shape: pl.BlockSpec(shape, lambda i: (0,) * len(shape))
    one = pl.BlockSpec((rb, W_A), lambda i: (i, 0))
    two = pl.BlockSpec((2, rb, W_A), lambda i: (0, i, 0))
    dot2 = pl.BlockSpec((2, rb, LANES), lambda i: (0, i, 0))
    s1 = jax.ShapeDtypeStruct((rows, W_A), F32)
    s2 = jax.ShapeDtypeStruct((2, rows, W_A), F32)
    sd = jax.ShapeDtypeStruct((2, rows, LANES), F32)
    return pl.pallas_call(
        _prep_kernel,
        grid=(rows // rb,),
        in_specs=[pl.BlockSpec((rb, RWKV_COLS), lambda i: (i, 0)),
                  full((2, LORA_W, W_A)), full((2, 1, W_A)), full((2, LORA_A, W_A)), full((2, 1, W_A)),
                  full((LORA_G, W_A)), full((1, W_A)), full((1, W_A)), full((1, W_A))],
        out_specs=[one, pl.BlockSpec((rb // SUB, HEAD, SUB, LANES), lambda i: (i, 0, 0, 0)), two, two, two, two,
                   dot2, dot2, one, one],
        out_shape=[s1, jax.ShapeDtypeStruct((rows // SUB, HEAD, SUB, LANES), F32), s2, s2, s2, s2, sd, sd, s1, s1],
        compiler_params=_cparams("parallel"),
        name="rwkv_prep",
    )(mixed, prm["w_up"], prm["w0"][:, None, :], prm["a_up"], prm["a0"][:, None, :], prm["g_up"],
      prm["k_k"][None], prm["k_a"][None], prm["r_k"][None])


def _scan_step(op, vrep, br, kr, s_scr, sa_scr, y_scr, raw_scr, unroll):
    grp = lax.broadcasted_iota(jnp.int32, (SUB, LANES), 1) // N_HEADS

    def accumulate(ig):
        p1 = [None] * KL
        p2 = [None] * KL
        for j in range(KH):
            kkj = op(0, j)
            wrj = op(1, j)
            for g in range(KL):
                s = s_scr[ig * KL + g, j]
                a1 = s * kkj
                a2 = s * wrj
                p1[g] = a1 if p1[g] is None else p1[g] + a1
                p2[g] = a2 if p2[g] is None else p2[g] + a2
        for g in range(KL):
            raw_scr[0, ig * KL + g] = p1[g]
            raw_scr[1, ig * KL + g] = p2[g]

    def fold_once(ig):
        for g in range(KL):
            for q in range(2):
                x = raw_scr[q, ig * KL + g]
                raw_scr[q, ig * KL + g] = x + pltpu.roll(x, N_HEADS, 1)

    def fold_twice(ig):
        yv = None
        for g in range(KL):
            i = ig * KL + g
            x1 = raw_scr[0, i]
            x2 = raw_scr[1, i]
            sa = x1 + pltpu.roll(x1, 2 * N_HEADS, 1)
            sa_scr[i] = sa
            y_g = x2 + pltpu.roll(x2, 2 * N_HEADS, 1) - sa * br + vrep(i) * kr
            yv = y_g if yv is None else jnp.where(grp == g, y_g, yv)
        y_scr[ig] = yv

    lag = 2
    for n in range(KH + 2 * lag):
        if 2 * lag <= n:
            fold_twice(n - 2 * lag)
        if lag <= n < KH + lag:
            fold_once(n - lag)
        if n < KH:
            accumulate(n)

    def update_group(ig, c):
        sa = [sa_scr[ig * KL + g] for g in range(KL)]
        vi = [vrep(ig * KL + g) for g in range(KL)]
        for j in range(KH):
            wj = op(2, j)
            bj = op(3, j)
            kdj = op(4, j)
            for g in range(KL):
                idx = ig * KL + g
                s_scr[idx, j] = s_scr[idx, j] * wj + (vi[g] * kdj - sa[g] * bj)
        return c

    lax.fori_loop(0, KH, update_group, 0, unroll=unroll)


_SCAN_SCRATCH = [pltpu.VMEM((HEAD, KH, SUB, LANES), F32), pltpu.VMEM((HEAD, SUB, LANES), F32),
                 pltpu.VMEM((KH, SUB, LANES), F32), pltpu.VMEM((2, HEAD, SUB, LANES), F32)]


def _scan_prompt_kernel(kk_ref, vrep_ref, w_ref, wr_ref, b_ref, kd_ref, br_ref, kr_ref, y_ref, sfin_ref,
                        s_scr, sa_scr, y_scr, raw_scr, *, tb, nt, unroll):
    d = pl.program_id(0)
    tblk = pl.program_id(2)

    @pl.when(tblk == 0)
    def _():
        s_scr[...] = jnp.zeros(s_scr.shape, F32)

    def step(tt, c):
        t = tt + d * (tb - 1 - 2 * tt)
        refs = (kk_ref.at[t, 0], wr_ref.at[0, t, 0], w_ref.at[0, t, 0], b_ref.at[0, t, 0], kd_ref.at[0, t, 0])
        op = lambda slot, j: refs[slot][:, j * LANES:(j + 1) * LANES]
        vrep = lambda i: vrep_ref[t, 0, i]
        _scan_step(op, vrep, br_ref[0, t, 0], kr_ref[0, t, 0], s_scr, sa_scr, y_scr, raw_scr, unroll)
        for j in range(KH):
            y_ref[0, t, 0, :, j * LANES:(j + 1) * LANES] = y_scr[j]
        return c

    lax.fori_loop(0, tb, step, 0)

    @pl.when(tblk == nt - 1)
    def _():
        sfin_ref[0, 0] = s_scr[...]


def _rwkv_scan_prompt(kk, vrep, w, wr, b, kd, br, kr, *, tb=16, unroll=2):
    t, nb = kk.shape[0], kk.shape[1]
    nt = t // tb

    def tmap(d, j):
        return j + d * (nt - 1 - 2 * j)

    shared = pl.BlockSpec((tb, 1, SUB, W_A), lambda d, bi, j: (tmap(d, j), bi, 0, 0))
    values = pl.BlockSpec((tb, 1, HEAD, SUB, LANES), lambda d, bi, j: (tmap(d, j), bi, 0, 0, 0))
    perdir = pl.BlockSpec((1, tb, 1, SUB, W_A), lambda d, bi, j: (d, tmap(d, j), bi, 0, 0))
    dots = pl.BlockSpec((1, tb, 1, SUB, LANES), lambda d, bi, j: (d, tmap(d, j), bi, 0, 0))
    state = pl.BlockSpec((1, 1, HEAD, KH, SUB, LANES), lambda d, bi, j: (d, bi, 0, 0, 0, 0))
    return pl.pallas_call(
        functools.partial(_scan_prompt_kernel, tb=tb, nt=nt, unroll=unroll),
        grid=(2, nb, nt),
        in_specs=[shared, values, perdir, perdir, perdir, perdir, dots, dots],
        out_specs=[perdir, state],
        out_shape=[jax.ShapeDtypeStruct((2, t, nb, SUB, W_A), F32),
                   jax.ShapeDtypeStruct((2, nb, HEAD, KH, SUB, LANES), F32)],
        scratch_shapes=_SCAN_SCRATCH,
        compiler_params=_cparams("parallel", "parallel", "arbitrary"),
        name="rwkv_scan_prompt",
    )(kk, vrep, w, wr, b, kd, br, kr)


def _scan_sample_kernel(kkf, kkb, vf, vb, wf, wb, wrf, wrb, bf, bb, kdf, kdb, brf, brb, krf, krb, s0_ref,
                        yf_ref, yb_ref, s_scr, sa_scr, y_scr, raw_scr, ops_scr, v_scr, *, tb, unroll):
    tblk = pl.program_id(0)
    half = SUB // 2

    @pl.when(tblk == 0)
    def _():
        s_scr[...] = s0_ref[...]

    def merge(a8, b8, odd):
        low = lax.broadcasted_iota(jnp.int32, a8.shape, 0) < half
        if odd:
            return pltpu.roll(jnp.where(low, b8, a8), half, 0)
        return jnp.where(low, a8, b8)

    def two_steps(t2, c):
        for odd in (False, True):
            tt = 2 * t2 + (1 if odd else 0)
            gf = t2
            gb = (tb - 1 - tt) // 2
            pairs = ((kkf.at[gf], kkb.at[gb]), (wrf.at[0, gf], wrb.at[0, gb]), (wf.at[0, gf], wb.at[0, gb]),
                     (bf.at[0, gf], bb.at[0, gb]), (kdf.at[0, gf], kdb.at[0, gb]))
            for slot, (fr, br_) in enumerate(pairs):
                for j in range(KH):
                    sl = slice(j * LANES, (j + 1) * LANES)
                    ops_scr[slot, j] = merge(fr[:, sl], br_[:, sl], odd)

            def stage_values(i, c2):
                v_scr[i] = merge(vf[gf, i], vb[gb, i], odd)
                return c2

            lax.fori_loop(0, HEAD, stage_values, 0, unroll=8)
            br = merge(brf[0, gf], brb[0, gb], odd)
            kr = merge(krf[0, gf], krb[0, gb], odd)
            _scan_step(lambda slot, j: ops_scr[slot, j], lambda i: v_scr[i], br, kr, s_scr, sa_scr, y_scr, raw_scr,
                       unroll)
            for j in range(KH):
                sl = slice(j * LANES, (j + 1) * LANES)
                y8 = y_scr[j]
                if odd:
                    y8 = pltpu.roll(y8, half, 0)
                    yf_ref[gf, half:, sl] = y8[half:]
                    yb_ref[gb, :half, sl] = y8[:half]
                else:
                    yf_ref[gf, :half, sl] = y8[:half]
                    yb_ref[gb, half:, sl] = y8[half:]
        return c

    lax.fori_loop(0, tb // 2, two_steps, 0)


def _rwkv_scan_sample(kk, vrep, w, wr, b, kd, br, kr, s0, *, tb=16, unroll=2):
    t2 = kk.shape[0]
    nt = 2 * t2 // tb
    g = tb // 2
    fw = lambda j: j
    bw = lambda j: nt - 1 - j
    sh = lambda m: pl.BlockSpec((g, SUB, W_A), lambda j: (m(j), 0, 0))
    val = lambda m: pl.BlockSpec((g, HEAD, SUB, LANES), lambda j: (m(j), 0, 0, 0))
    pd = lambda d, m: pl.BlockSpec((1, g, SUB, W_A), lambda j: (d, m(j), 0, 0))
    dt = lambda d, m: pl.BlockSpec((1, g, SUB, LANES), lambda j: (d, m(j), 0, 0))
    st = pl.BlockSpec((HEAD, KH, SUB, LANES), lambda j: (0, 0, 0, 0))
    return pl.pallas_call(
        functools.partial(_scan_sample_kernel, tb=tb, unroll=unroll),
        grid=(nt,),
        in_specs=[sh(fw), sh(bw), val(fw), val(bw), pd(0, fw), pd(1, bw), pd(0, fw), pd(1, bw), pd(0, fw),
                  pd(1, bw), pd(0, fw), pd(1, bw), dt(0, fw), dt(1, bw), dt(0, fw), dt(1, bw), st],
        out_specs=[sh(fw), sh(bw)],
        out_shape=[jax.ShapeDtypeStruct((t2, SUB, W_A), F32), jax.ShapeDtypeStruct((t2, SUB, W_A), F32)],
        scratch_shapes=_SCAN_SCRATCH + [pltpu.VMEM((5, KH, SUB, LANES), F32), pltpu.VMEM((HEAD, SUB, LANES), F32)],
        compiler_params=_cparams("arbitrary"),
        name="rwkv_scan_sample",
    )(kk, kk, vrep, vrep, w, w, wr, wr, b, b, kd, kd, br, br, kr, kr, s0)


def _post_kernel(y0_ref, y1_ref, bonus_ref, g_ref, lg_ref, lb_ref, o_ref):
    y = y0_ref[...] + y1_ref[...]
    mu = jnp.tile(_head_sum(y) * (1.0 / HEAD), (1, KH))
    c = y - mu
    var = jnp.tile(_head_sum(c * c) * (1.0 / HEAD), (1, KH))
    yn = c * lax.rsqrt(var + GN_EPS) * lg_ref[...] + lb_ref[...]
    o_ref[...] = ((yn + bonus_ref[...]) * g_ref[...]).astype(o_ref.dtype)


def _rwkv_post(y0, y1, bonus, g, lnx_g, lnx_b, *, rb=256):
    rows = y0.shape[0]
    one = pl.BlockSpec((rb, W_A), lambda i: (i, 0))
    vec = pl.BlockSpec((1, W_A), lambda i: (0, 0))
    return pl.pallas_call(
        _post_kernel, grid=(rows // rb,), in_specs=[one, one, one, one, vec, vec], out_specs=one,
        out_shape=jax.ShapeDtypeStruct((rows, W_A), BF16),
        compiler_params=_cparams("parallel"), name="rwkv_post",
    )(y0, y1, bonus, g, lnx_g[None], lnx_b[None])


CONF_CHUNK = 512
CONF_NCH = W_B // CONF_CHUNK


def _conf_kernel(prev_ref, cur_ref, next_ref, w_ref, b_ref, lg_ref, lb_ref, o_ref, ext_scr, acc_scr,
                 *, nb, blocks):
    i = pl.program_id(0)
    rb = cur_ref.shape[0]
    has_prev = (i > 0).astype(F32)
    has_next = (i < blocks - 1).astype(F32)
    for c in range(CONF_NCH):
        cs = slice(c * CONF_CHUNK, (c + 1) * CONF_CHUNK)
        ext_scr[c, :rb] = prev_ref[:, cs] * has_prev
        ext_scr[c, rb:2 * rb] = cur_ref[:, cs]
        ext_scr[c, 2 * rb:] = next_ref[:, cs] * has_next
    pad = (CONV_B - 1) // 2

    def chunk(c, sums):
        acc = None
        for j in range(CONV_B):
            lo = rb + (j - pad) * nb
            term = ext_scr[c, lo:lo + rb] * w_ref[c, j:j + 1]
            acc = term if acc is None else acc + term
        acc = acc + b_ref[c]
        acc_scr[c] = acc
        return sums + jnp.sum(acc, axis=-1, keepdims=True)

    mu = lax.fori_loop(0, CONF_NCH, chunk, jnp.zeros((rb, 1), F32)) * (1.0 / W_B)
    var = jnp.zeros((rb, 1), F32)
    for c in range(CONF_NCH):
        d = acc_scr[c] - mu
        var = var + jnp.sum(d * d, axis=-1, keepdims=True)
    scale = lax.rsqrt(var * (1.0 / W_B) + LN_EPS)
    for c in range(CONF_NCH):
        y = (acc_scr[c] - mu) * scale * lg_ref[c] + lb_ref[c]
        o_ref[:, c * CONF_CHUNK:(c + 1) * CONF_CHUNK] = (y * _sigmoid(y)).astype(o_ref.dtype)


def _conformer(glu, prm, *, rb=256):
    chunked = lambda a: jnp.swapaxes(a.reshape(a.shape[0], CONF_NCH, CONF_CHUNK), 0, 1)
    outs = []
    for lo, nb, t in RUNS:
        rows = nb * t
        blocks = rows // rb
        base = lo // rb
        assert (CONV_B - 1) // 2 * nb <= rb
        spec = lambda f: pl.BlockSpec((rb, W_B), f)
        vec = lambda n: pl.BlockSpec((CONF_NCH, n, CONF_CHUNK), lambda i: (0, 0, 0))
        outs.append(pl.pallas_call(
            functools.partial(_conf_kernel, nb=nb, blocks=blocks),
            grid=(blocks,),
            in_specs=[spec(lambda i: (base + jnp.maximum(i - 1, 0), 0)), spec(lambda i: (base + i, 0)),
                      spec(lambda i: (base + jnp.minimum(i + 1, blocks - 1), 0)),
                      vec(CONV_B), vec(1), vec(1), vec(1)],
            out_specs=pl.BlockSpec((rb, W_B), lambda i: (i, 0)),
            out_shape=jax.ShapeDtypeStruct((rows, W_B), BF16),
            scratch_shapes=[pltpu.VMEM((CONF_NCH, 3 * rb, CONF_CHUNK), F32),
                            pltpu.VMEM((CONF_NCH, rb, CONF_CHUNK), F32)],
            compiler_params=_cparams("parallel"),
            name="conformer_conv",
        )(glu, glu, glu, chunked(prm["dw_w"]), chunked(prm["dw_b"][None]), chunked(prm["ln_g"][None]),
          chunked(prm["ln_b"][None])))
    return jnp.concatenate(outs, axis=0)


def _hyena_kernel(x0_ref, x1_ref, v_ref, w0_ref, w1_ref, wv_ref, b0_ref, b1_ref, bv_ref, bias_ref, kf_ref,
                  fwd_ref, inv_ref, o_ref):
    length = x0_ref.shape[0]
    row = lax.broadcasted_iota(jnp.int32, x0_ref.shape, 0)

    def short(u_ref, w_ref, b_ref):
        u = u_ref[...]
        prev = jnp.where(row == 0, 0.0, pltpu.roll(u, 1, 0))
        nxt = jnp.where(row == length - 1, 0.0, pltpu.roll(u, length - 1, 0))
        return prev * w_ref[0:1] + u * w_ref[1:2] + nxt * w_ref[2:3] + b_ref[...]

    x1 = short(x1_ref, w1_ref, b1_ref)
    v = short(v_ref, wv_ref, bv_ref)
    z = v * x1
    zf = jnp.dot(fwd_ref[...], z.astype(BF16), preferred_element_type=F32)
    zc, zs = zf[:length], zf[length:]
    kc, ks = kf_ref[:length], kf_ref[length:]
    first = row == 0
    yc = jnp.where(first, zc * kc, zc * kc - zs * ks)
    ys = jnp.where(first, zs * ks, zc * ks + zs * kc)
    yf = jnp.concatenate([yc, ys], axis=0).astype(BF16)
    y = jnp.dot(inv_ref[...], yf, preferred_element_type=F32)
    x0 = short(x0_ref, w0_ref, b0_ref)
    o_ref[...] = ((y + bias_ref[...] * z) * x0).astype(o_ref.dtype)


def _dft_tables(length):
    n = 2 * length
    kidx = jnp.arange(length, dtype=jnp.int32)[:, None]
    sidx = jnp.arange(n, dtype=jnp.int32)[None, :]
    ang = (2.0 * math.pi / n) * ((kidx * sidx) % n).astype(F32)
    cos = jnp.cos(ang)
    sin = jnp.where(kidx == 0, jnp.where(sidx % 2 == 0, 1.0, -1.0).astype(F32), jnp.sin(ang))
    fwd = jnp.concatenate([cos, sin], axis=0)
    scale = jnp.where(kidx == 0, 1.0 / n, 2.0 / n)
    inv = jnp.concatenate([(cos * scale)[:, :length].T, (sin * scale)[:, :length].T], axis=1)
    return fwd, inv


def _hyena_filters(length, prm):
    hi = lax.Precision.HIGHEST
    t = jnp.linspace(0.0, 1.0, length, dtype=F32)[:, None]
    bands = (HY_EMB_DIM - 1) // 2
    ang = (2.0 * math.pi / length) * jnp.arange(length, dtype=F32)[:, None] \
        * jnp.linspace(1e-4, bands - 1, bands, dtype=F32)[None, :]
    z = jnp.concatenate([t, jnp.cos(ang), -jnp.sin(ang)], axis=-1)
    fr = prm["f_freq"]
    h = jnp.sin(fr * (jnp.dot(z, prm["f_w1"], precision=hi) + prm["f_b1"]))
    h = jnp.sin(fr * (jnp.dot(h, prm["f_w2"], precision=hi) + prm["f_b2"]))
    h = jnp.sin(fr * (jnp.dot(h, prm["f_w3"], precision=hi) + prm["f_b3"]))
    h = jnp.dot(h, prm["f_w4"], precision=hi).reshape(length, 2, D_MODEL)
    deltas = jnp.abs(jnp.linspace(HY_MIN_DECAY, HY_MAX_DECAY, D_MODEL, dtype=F32))
    h = h * jnp.exp(-t * deltas)[:, None, :]
    h = h / (jnp.sum(jnp.abs(h), axis=(0, 1)) + 1e-6)
    return h[:, 0], h[:, 1]


def _hyena(u, prm):
    outs = []
    for lo, nb, t in RUNS:
        tn = 1024 if t <= 256 else 256
        per = D_MODEL // tn
        hf, hb = _hyena_filters(t, prm)
        fwd, inv = _dft_tables(t)
        kern = jnp.concatenate([hf, jnp.zeros((1, D_MODEL), F32), jnp.flip(hb[1:], axis=0)], axis=0)
        kf = _mm(fwd, kern, tm=min(2 * t, 1024), tn=1024)
        first = lo // t
        part = lambda p: pl.BlockSpec((t, tn), lambda b, j: (first + b, p * per + j))
        wpart = lambda p, n: pl.BlockSpec((n, tn), lambda b, j: (0, p * per + j))
        chan = lambda n: pl.BlockSpec((n, tn), lambda b, j: (0, j))
        const = lambda s: pl.BlockSpec(s, lambda b, j: (0, 0))
        out = pl.pallas_call(
            _hyena_kernel,
            grid=(nb, per),
            in_specs=[part(0), part(1), part(2), wpart(0, HY_SHORT), wpart(1, HY_SHORT), wpart(2, HY_SHORT),
                      wpart(0, 1), wpart(1, 1), wpart(2, 1), chan(1), chan(2 * t),
                      const((2 * t, t)), const((t, 2 * t))],
            out_specs=pl.BlockSpec((t, tn), lambda b, j: (b, j)),
            out_shape=jax.ShapeDtypeStruct((nb * t, D_MODEL), BF16),
            compiler_params=_cparams("parallel", "arbitrary"),
            name="hyena_conv",
        )(u, u, u, prm["sc_w"], prm["sc_w"], prm["sc_w"], prm["sc_b"][None], prm["sc_b"][None],
          prm["sc_b"][None], prm["bias"][None], kf, fwd[:, :t].astype(BF16), inv.astype(BF16))
        outs.append(out)
    return jnp.concatenate(outs, axis=0)


def _moe_up_kernel(x_ref, wg_ref, wu_ref, h_ref):
    x = x_ref[0]
    g = jnp.dot(x, wg_ref[0].astype(BF16), preferred_element_type=F32)
    u = jnp.dot(x, wu_ref[0].astype(BF16), preferred_element_type=F32)
    h_ref[0] = (g * _sigmoid(g) * u).astype(h_ref.dtype)


def _moe_down_kernel(h_ref, wd_ref, gate_ref, op_ref, os_ref):
    res = jnp.dot(h_ref[0], wd_ref[0].astype(BF16), preferred_element_type=F32) * gate_ref[0]
    n_p = op_ref.shape[0] * op_ref.shape[1]
    op_ref[...] = res[:n_p].reshape(op_ref.shape)
    os_ref[...] = res[n_p:].reshape(os_ref.shape)


def _moe_experts(xg, gate, wg, wu, wd, layer, *, tf=256, tn=512):
    e, rows, _ = xg.shape
    caps = [EC_FACTOR * t // N_EXPERTS for _, _, t in RUNS]
    h = pl.pallas_call(
        _moe_up_kernel,
        grid=(e, D_EXPERT // tf),
        in_specs=[pl.BlockSpec((1, rows, D_MODEL), lambda ei, f: (ei, 0, 0)),
                  pl.BlockSpec((None, 1, D_MODEL, tf), lambda ei, f: (layer, ei, 0, f)),
                  pl.BlockSpec((None, 1, D_MODEL, tf), lambda ei, f: (layer, ei, 0, f))],
        out_specs=pl.BlockSpec((1, rows, tf), lambda ei, f: (ei, 0, f)),
        out_shape=jax.ShapeDtypeStruct((e, rows, D_EXPERT), BF16),
        compiler_params=_cparams("parallel", "arbitrary"),
        name="moe_up",
    )(xg, wg, wu)
    seq_major = lambda nb, cap: pl.BlockSpec((nb, None, cap, tn), lambda ei, n: (0, ei, 0, n))
    return pl.pallas_call(
        _moe_down_kernel,
        grid=(e, D_MODEL // tn),
        in_specs=[pl.BlockSpec((1, rows, D_EXPERT), lambda ei, n: (ei, 0, 0)),
                  pl.BlockSpec((None, 1, D_EXPERT, tn), lambda ei, n: (layer, ei, 0, n)),
                  pl.BlockSpec((1, rows, 1), lambda ei, n: (ei, 0, 0))],
        out_specs=[seq_major(N_PROMPT, caps[0]), seq_major(N_SAMPLE, caps[1])],
        out_shape=[jax.ShapeDtypeStruct((N_PROMPT, e, caps[0], D_MODEL), F32),
                   jax.ShapeDtypeStruct((N_SAMPLE, e, caps[1], D_MODEL), F32)],
        compiler_params=_cparams("parallel", "arbitrary"),
        name="moe_down",
    )(h, wd, gate)


def _combine_kernel(idx_ref, o_ref, g_ref, x_ref, y_ref, onehot_scr):
    @pl.when(pl.program_id(1) == 0)
    def _():
        tok = lax.broadcasted_iota(jnp.int32, onehot_scr.shape, 0)
        onehot_scr[...] = jnp.where(tok == idx_ref[...], 1.0, 0.0).astype(BF16)

    onehot = onehot_scr[...]
    o = o_ref[...]
    hi = o.astype(BF16)
    lo = (o - hi.astype(F32)).astype(BF16)
    mix = jnp.dot(onehot, hi, preferred_element_type=F32) + jnp.dot(onehot, lo, preferred_element_type=F32)
    y_ref[...] = x_ref[...] + g_ref[...] * mix


def _moe_combine(x, gate_rows, outs, idxs, *, tn=512):
    for run, (lo, nb, t) in enumerate(RUNS):
        o = outs[run].reshape(nb, -1, D_MODEL)
        slots = o.shape[1]
        first = lo // t
        cond = (lambda b: 0) if run == 0 else (lambda b: 1 + b)
        rows = pl.BlockSpec((t, tn), lambda b, j: (first + b, j))
        x = pl.pallas_call(
            _combine_kernel,
            grid=(nb, D_MODEL // tn),
            in_specs=[pl.BlockSpec((None, 1, slots), lambda b, j: (b, 0, 0)),
                      pl.BlockSpec((None, slots, tn), lambda b, j: (b, 0, j)),
                      pl.BlockSpec((None, 1, tn), lambda b, j: (cond(b), 0, j)),
                      rows],
            out_specs=rows,
            out_shape=jax.ShapeDtypeStruct((N_TOK, D_MODEL), F32),
            scratch_shapes=[pltpu.VMEM((t, slots), BF16)],
            input_output_aliases={3: 0},
            compiler_params=_cparams("parallel", "arbitrary"),
            name="moe_combine",
        )(idxs[run].reshape(nb, 1, slots), o, gate_rows[:, None, :], x)
    return x


def _ec_moe(x, gate_rows, h, logits, wg, wu, wd, layer):
    aff = jax.nn.softmax(logits, axis=-1)
    gates, rows, idxs = [], [], []
    for lo, nb, t in RUNS:
        cap = EC_FACTOR * t // N_EXPERTS
        a = jnp.swapaxes(aff[lo:lo + nb * t].reshape(nb, t, N_EXPERTS), 1, 2)
        gate, idx = lax.top_k(a, cap)
        idxs.append(idx)
        glob = lo + idx + (jnp.arange(nb, dtype=jnp.int32) * t)[:, None, None]
        rows.append(jnp.swapaxes(glob, 0, 1).reshape(N_EXPERTS, nb * cap))
        gates.append(jnp.swapaxes(gate, 0, 1).reshape(N_EXPERTS, nb * cap))
    rows = jnp.concatenate(rows, axis=1)
    gate = jnp.concatenate(gates, axis=1)
    outs = _moe_experts(h[rows], gate[..., None], wg, wu, wd, layer)
    return _moe_combine(x, gate_rows, outs, idxs)


def _key_major(x):
    return jnp.swapaxes(x.reshape(x.shape[:-1] + (N_HEADS, HEAD)), -1, -2).reshape(x.shape)


def _to_time_major(x):
    return jnp.concatenate([jnp.swapaxes(x[lo:lo + nb * t].reshape(nb, t, -1), 0, 1).reshape(nb * t, -1)
                            for lo, nb, t in RUNS], axis=0)


def _to_seq_major(x):
    return jnp.concatenate([jnp.swapaxes(x[lo:lo + nb * t].reshape(t, nb, -1), 0, 1).reshape(nb * t, -1)
                            for lo, nb, t in RUNS], axis=0)


def _shift_steps(p, steps):
    if steps > 0:
        return jnp.pad(p[:-steps], ((steps, 0), (0, 0), (0, 0)))
    return jnp.pad(p[-steps:], ((0, -steps), (0, 0), (0, 0)))


def _token_shift_mix(p, quad, mu):
    outs = []
    for lo, nb, t in RUNS:
        q = p[lo:lo + nb * t].reshape(t, nb, RWKV_COLS)
        if t == T_PROMPT:
            shifted = jnp.where(quad % 2 == 0, _shift_steps(q, 1), _shift_steps(q, -1))
        else:
            col = (lax.broadcasted_iota(jnp.int32, (t, 1, 1), 0) % GRID_W)
            left = jnp.where(col == 0, 0.0, _shift_steps(q, 1))
            right = jnp.where(col == GRID_W - 1, 0.0, _shift_steps(q, -1))
            shifted = jnp.where(quad == 0, left, jnp.where(quad == 1, right, jnp.where(
                quad == 2, _shift_steps(q, GRID_W), _shift_steps(q, -GRID_W))))
        outs.append((q + (shifted - q) * mu).reshape(nb * t, RWKV_COLS))
    return outs


def _rwkv_mixer(mixed, s0_sample, prm):
    kk, vrep, w, wr, b, kd, br, kr, g, bonus = _rwkv_prep(mixed[0], prm)
    nbt = N_PROMPT // SUB
    tile = lambda x: x.reshape(x.shape[:-2] + (T_PROMPT, nbt, SUB, x.shape[-1]))
    y_p, s_p = _rwkv_scan_prompt(tile(kk), vrep.reshape(T_PROMPT, nbt, HEAD, SUB, LANES), tile(w), tile(wr),
                                 tile(b), tile(kd), tile(br), tile(kr))
    y_p = y_p.reshape(2, ROWS_P, W_A)
    y_ap = _rwkv_post(y_p[0], y_p[1], bonus, g, prm["lnx_g"], prm["lnx_b"])
    s_p = s_p.reshape(2, nbt, HEAD, KH, SUB, KL, N_HEADS)
    s_p = jnp.transpose(s_p, (1, 4, 0, 6, 2, 3, 5)).reshape(N_PROMPT, 2, N_HEADS, HEAD, HEAD)
    kk, vrep, w, wr, b, kd, br, kr, g, bonus = _rwkv_prep(mixed[1], prm)
    tile = lambda x: x.reshape(x.shape[:-2] + (T_SAMPLE // 2, SUB, x.shape[-1]))
    s0 = s0_sample.reshape(N_SAMPLE, 2, N_HEADS, HEAD, KH, KL)
    s0 = jnp.transpose(s0, (3, 4, 1, 0, 5, 2)).reshape(HEAD, KH, SUB, LANES)
    yf, yb = _rwkv_scan_sample(tile(kk), vrep, tile(w), tile(wr), tile(b), tile(kd), tile(br), tile(kr), s0)
    y_as = _rwkv_post(yf.reshape(ROWS_S, W_A), yb.reshape(ROWS_S, W_A), bonus, g, prm["lnx_g"], prm["lnx_b"])
    return jnp.concatenate([y_ap, y_as], axis=0), s_p


def kernel(x_prompt, x_sample, state_rwkv, c, c_ctx, norm_g, mod_w, mod_b, ab_w_in, ab_mu, ab_w_up, ab_w0, ab_a_up, ab_a0, ab_g_up, ab_k_k, ab_k_a, ab_r_k, ab_lnx_g, ab_lnx_b, ab_dw_w, ab_dw_b, ab_ln_g, ab_ln_b, ab_w_out, hy_w_in, hy_sc_w, hy_sc_b, hy_f_w1, hy_f_b1, hy_f_w2, hy_f_b2, hy_f_w3, hy_f_b3, hy_f_w4, hy_f_freq, hy_bias, hy_w_out, moe_router, moe_w_gate, moe_w_up, moe_w_down, final_g):
    x = jnp.concatenate([x_prompt.reshape(ROWS_P, D_MODEL), x_sample.reshape(ROWS_S, D_MODEL)], axis=0)
    cond = jnp.concatenate([c_ctx[None], c, jnp.zeros((N_COND - 1 - N_SAMPLE, D_MODEL), F32)], axis=0)
    chan = jnp.arange(RWKV_COLS, dtype=jnp.int32)
    quad = jnp.concatenate([_key_major(chan[:3 * W_A].reshape(3, W_A)).reshape(-1), chan[3 * W_A:]]) % 4
    states = []
    for l in range(DEPTH):
        mod = _modulation(cond, mod_w, mod_b[l], l)
        sh1, sc1, g1, sh2, sc2, g2 = jnp.split(mod, 6, axis=-1)
        h = _norm_mod(x, norm_g[l, 0], sc1, sh1, out_dtype=BF16)
        i = l // 2
        if l % 2 == 0:
            km3 = lambda a: jnp.concatenate([_key_major(a[..., :3 * W_A].reshape(a.shape[:-1] + (3, W_A))).reshape(
                a.shape[:-1] + (3 * W_A,)), a[..., 3 * W_A:]], axis=-1)
            w_rwkv = km3(ab_w_in[i][:, :RWKV_COLS])
            prm = dict(w_up=_key_major(ab_w_up[i]), w0=_key_major(ab_w0[i]), a_up=_key_major(ab_a_up[i]),
                       a0=_key_major(ab_a0[i]), g_up=_key_major(ab_g_up[i]), k_k=_key_major(ab_k_k[i]),
                       k_a=_key_major(ab_k_a[i]), r_k=_key_major(ab_r_k[i].reshape(W_A)),
                       lnx_g=_key_major(ab_lnx_g[i]), lnx_b=_key_major(ab_lnx_b[i]),
                       dw_w=ab_dw_w[i], dw_b=ab_dw_b[i], ln_g=ab_ln_g[i], ln_b=ab_ln_b[i])
            h_tm = _to_time_major(h)
            proj = _mm(h_tm, w_rwkv, tm=2048, tn=256)
            mixed = _token_shift_mix(proj, quad, km3(ab_mu[i]))
            y_a, s_fin = _rwkv_mixer(mixed, state_rwkv[:, i], prm)
            states.append(s_fin)
            glu = _glu_mm(h_tm, ab_w_in, layer=i, col_a=RWKV_COLS, col_b=RWKV_COLS + W_B, n=W_B, tm=1024, tn=256)
            cv = _conformer(glu, prm)
            w_out = jnp.concatenate([jnp.swapaxes(ab_w_out[i][:W_A].reshape(N_HEADS, HEAD, D_MODEL), 0, 1).reshape(
                W_A, D_MODEL), ab_w_out[i][W_A:]], axis=0)
            x = _mm_residual(_to_seq_major(jnp.concatenate([y_a, cv], axis=-1)), w_out, x, g1, tn=256)
        else:
            u = _mm(h, hy_w_in, layer=i, tm=2048, tn=256)
            prm = dict(sc_w=hy_sc_w[i], sc_b=hy_sc_b[i], f_w1=hy_f_w1[i], f_b1=hy_f_b1[i], f_w2=hy_f_w2[i],
                       f_b2=hy_f_b2[i], f_w3=hy_f_w3[i], f_b3=hy_f_b3[i], f_w4=hy_f_w4[i], f_freq=hy_f_freq[i],
                       bias=hy_bias[i])
            x = _mm_residual(_hyena(u, prm), hy_w_out, x, g1, tn=256, layer=i)
        h2, logits = _norm_mod(x, norm_g[l, 1], sc2, sh2, out_dtype=BF16, router=moe_router[l])
        x = _ec_moe(x, g2, h2, logits, moe_w_gate, moe_w_up, moe_w_down, l)
    zeros = jnp.zeros((N_COND, D_MODEL), F32)
    y = _norm_mod(x, final_g, zeros, zeros, out_dtype=F32)
    y_prompt = y[:ROWS_P].reshape(N_PROMPT, T_PROMPT, D_MODEL)
    y_sample = y[ROWS_P:].reshape(N_SAMPLE, T_SAMPLE, D_MODEL)
    new_state = jnp.stack(states, axis=1).astype(x_prompt.dtype)
    return (y_prompt, y_sample, new_state)
```
